```python
import jax, jax.numpy as jnp
from jax import lax
import numpy as np

D_MODEL = 2048
BATCH = 1
SEQ = 8192
DEPTH = 2
DEC_BATCH = 2
DEC_SEQ = 16384
PAST_LEN = 128

N_MIXERS = 2
EXPAND = 2
D_INNER = EXPAND * D_MODEL
CHUNK_A = 128
N_GROUPS_A = 16
GROUP_DIM_A = D_INNER // N_GROUPS_A
GLA_HEADS = 4
GLA_KEY_DIM = D_MODEL // 2
GLA_HEAD_K = GLA_KEY_DIM // GLA_HEADS
GLA_HEAD_V = D_INNER // GLA_HEADS
GATE_RANK = 16
GATE_TAU = 16.0
CHUNK_B = 64
N_A_LAYERS = (DEPTH + 1) // 2
N_B_LAYERS = DEPTH // 2
DEEPNORM_ALPHA = (2 * DEPTH) ** 0.25
DEEPNORM_BETA = (8 * DEPTH) ** -0.25
LN_EPS = 1e-5
RMS_EPS = 1e-6
B_IN_COLS = 2 * GLA_KEY_DIM + 2 * D_INNER + 2 * GATE_RANK

kernel_name = "hybrid_bidir_gmlp_gla_deepnorm"


def layer_norm(x, g, b):
    xf = x.astype(jnp.float32)
    mu = jnp.mean(xf, axis=-1, keepdims=True)
    xc = xf - mu
    var = jnp.mean(xc * xc, axis=-1, keepdims=True)
    return (xc * lax.rsqrt(var + LN_EPS) * g.astype(jnp.float32) + b.astype(jnp.float32)).astype(x.dtype)


def mixer_a(x, w_in, ln_v_g, ln_v_b, w_s, b_s, w_out):
    B, T, _ = x.shape
    h = x @ w_in
    u, v, z = jnp.split(h, 3, axis=-1)
    u = jax.nn.gelu(u)
    v = layer_norm(jax.nn.gelu(v), ln_v_g, ln_v_b)
    vc = v.reshape(B, T // CHUNK_A, CHUNK_A, N_GROUPS_A, GROUP_DIM_A)
    s = jnp.einsum('gts,bnsgc->bntgc', w_s, vc) + b_s.T[None, None, :, :, None]
    s = s.reshape(B, T, D_INNER)
    y = u * s * jax.nn.silu(z)
    return y @ w_out


def gla_chunked(q, k, v, g, exclusive):
    B, T, H, dk = q.shape
    dv = v.shape[-1]
    N = T // CHUNK_B

    def to_chunks(a):
        return a.astype(jnp.float32).reshape(B, N, CHUNK_B, H, a.shape[-1]).transpose(1, 0, 3, 2, 4)

    qc, kc, vc, gc = to_chunks(q), to_chunks(k), to_chunks(v), to_chunks(g)
    G = jnp.cumsum(gc, axis=3)
    mask = jnp.tril(jnp.ones((CHUNK_B, CHUNK_B), dtype=bool), k=-1 if exclusive else 0)
    mid = CHUNK_B // 2

    def step(S, inp):
        qn, kn, vn, Gn = inp
        Gmid = Gn[:, :, mid:mid + 1]
        Glast = Gn[:, :, -1:]
        a = jnp.einsum('bhtd,bhsd->bhts', qn * jnp.exp(Gn - Gmid), kn * jnp.exp(Gmid - Gn))
        a = jnp.where(mask, a, 0.0)
        o = jnp.einsum('bhts,bhse->bhte', a, vn) + jnp.einsum('bhtd,bhde->bhte', qn * jnp.exp(Gn), S)
        S = jnp.exp(Glast)[:, :, 0, :, None] * S + jnp.einsum('bhsd,bhse->bhde', kn * jnp.exp(Glast - Gn), vn)
        return S, o

    S0 = jnp.zeros((B, H, dk, dv), jnp.float32)
    _, o = lax.scan(step, S0, (qc, kc, vc, G))
    return o.transpose(1, 0, 3, 2, 4).reshape(B, T, H, dv)


def mixer_b(x, w_in, w_g2, b_g, gn_g, w_out):
    B, T, _ = x.shape
    h = x @ w_in
    K, E = GLA_KEY_DIM, D_INNER
    q, k, v, z, gl = jnp.split(h, [K, 2 * K, 2 * K + E, 2 * K + 2 * E], axis=-1)
    q = q.reshape(B, T, GLA_HEADS, GLA_HEAD_K) * (GLA_HEAD_K ** -0.5)
    k = k.reshape(B, T, GLA_HEADS, GLA_HEAD_K)
    v = v.reshape(B, T, GLA_HEADS, GLA_HEAD_V)
    gl = gl.reshape(B, T, 2, GATE_RANK).astype(jnp.float32)
    glog = jax.nn.log_sigmoid(jnp.einsum('btdr,drk->btdk', gl, w_g2.astype(jnp.float32))
                              + b_g.astype(jnp.float32)) / GATE_TAU
    g_f = glog[:, :, 0].reshape(B, T, GLA_HEADS, GLA_HEAD_K)
    g_b = glog[:, :, 1].reshape(B, T, GLA_HEADS, GLA_HEAD_K)
    o_f = gla_chunked(q, k, v, g_f, exclusive=False)
    flip = lambda a: jnp.flip(a, axis=1)
    o_b = flip(gla_chunked(flip(q), flip(k), flip(v), flip(g_b), exclusive=True))
    o = o_f + o_b
    o = o * lax.rsqrt(jnp.mean(o * o, axis=-1, keepdims=True) + RMS_EPS)
    o = o * gn_g.astype(jnp.float32).reshape(GLA_HEADS, GLA_HEAD_V)
    y = o.reshape(B, T, E).astype(x.dtype) * jax.nn.silu(z)
    return y @ w_out


def trunk(x, w_in_a, ln_v_g_a, ln_v_b_a, w_s_a, b_s_a, w_out_a,
          w_in_b, w_g2_b, b_g_b, gn_g_b, w_out_b, ln_g, ln_b):
    for i in range(DEPTH):
        j = i // N_MIXERS
        if i % N_MIXERS == 0:
            f = mixer_a(x, w_in_a[j], ln_v_g_a[j], ln_v_b_a[j], w_s_a[j], b_s_a[j], w_out_a[j])
        else:
            f = mixer_b(x, w_in_b[j], w_g2_b[j], b_g_b[j], gn_g_b[j], w_out_b[j])
        x = layer_norm(DEEPNORM_ALPHA * x + f, ln_g[i], ln_b[i])
    return x


def setup_inputs(seed: int = 0) -> dict:
    key = jax.random.key(seed)
    ks = jax.random.split(key, 16)
    nrm = jax.random.normal
    f32 = jnp.float32
    return {
        "x_prompt": nrm(ks[0], (BATCH, SEQ, D_MODEL), f32),
        "x_sample": nrm(ks[1], (DEC_BATCH, DEC_SEQ, D_MODEL), f32),
        "w_in_a": nrm(ks[2], (N_A_LAYERS, D_MODEL, 3 * D_INNER), f32) * D_MODEL ** -0.5,
        "ln_v_g_a": 1.0 + 0.1 * nrm(ks[3], (N_A_LAYERS, D_INNER), f32),
        "ln_v_b_a": 0.02 * nrm(ks[4], (N_A_LAYERS, D_INNER), f32),
        "w_s_a": nrm(ks[5], (N_A_LAYERS, N_GROUPS_A, CHUNK_A, CHUNK_A), f32) * CHUNK_A ** -0.5,
        "b_s_a": 1.0 + 0.1 * nrm(ks[6], (N_A_LAYERS, N_GROUPS_A, CHUNK_A), f32),
        "w_out_a": nrm(ks[7], (N_A_LAYERS, D_INNER, D_MODEL), f32) * (D_INNER ** -0.5 * DEEPNORM_BETA),
        "w_in_b": nrm(ks[8], (N_B_LAYERS, D_MODEL, B_IN_COLS), f32) * D_MODEL ** -0.5,
        "w_g2_b": nrm(ks[9], (N_B_LAYERS, 2, GATE_RANK, GLA_KEY_DIM), f32) * GATE_RANK ** -0.5,
        "b_g_b": 0.1 * nrm(ks[10], (N_B_LAYERS, 2, GLA_KEY_DIM), f32),
        "gn_g_b": 1.0 + 0.1 * nrm(ks[11], (N_B_LAYERS, D_INNER), f32),
        "w_out_b": nrm(ks[12], (N_B_LAYERS, D_INNER, D_MODEL), f32) * (D_INNER ** -0.5 * DEEPNORM_BETA),
        "ln_g": 1.0 + 0.1 * nrm(ks[13], (DEPTH, D_MODEL), f32),
        "ln_b": 0.02 * nrm(ks[14], (DEPTH, D_MODEL), f32),
    }


def reference(x_prompt, x_sample, w_in_a, ln_v_g_a, ln_v_b_a, w_s_a, b_s_a, w_out_a,
              w_in_b, w_g2_b, b_g_b, gn_g_b, w_out_b, ln_g, ln_b):
    y_prompt = trunk(x_prompt, w_in_a, ln_v_g_a, ln_v_b_a, w_s_a, b_s_a, w_out_a,
                     w_in_b, w_g2_b, b_g_b, gn_g_b, w_out_b, ln_g, ln_b)
    y_sample = trunk(x_sample, w_in_a, ln_v_g_a, ln_v_b_a, w_s_a, b_s_a, w_out_a,
                     w_in_b, w_g2_b, b_g_b, gn_g_b, w_out_b, ln_g, ln_b)
    return (y_prompt, y_sample)
```

```python
import functools

import jax
import jax.numpy as jnp
from jax import lax
from jax.experimental import pallas as pl
from jax.experimental.pallas import tpu as pltpu

F32 = jnp.float32
BF16 = jnp.bfloat16

CHUNK_A = 128
N_GROUPS_A = 16
GLA_HEADS = 4
GATE_RANK = 16
GATE_TAU = 16.0
SUB_B = 64
LN_EPS = 1e-5
RMS_EPS = 1e-6

TM_A = 512
CW_A = 512
TM_P = 512
CW_P = 512
TB_G = 512
MACRO_B = 128
CUM_ROWS = 256
VMEM_LIMIT = 56 * 1024 * 1024


def _dot(a, b):
    return jnp.dot(a, b, preferred_element_type=F32)


def _dot_nt(a, b):
    return lax.dot_general(a, b, (((1,), (1,)), ((), ())), preferred_element_type=F32)


def _dot_tn(a, b):
    return lax.dot_general(a, b, (((0,), (0,)), ((), ())), preferred_element_type=F32)


def _gelu_tanh(x):
    c = 0.7978845608028654
    return 0.5 * x * (1.0 + jnp.tanh(c * (x + 0.044715 * (x * x * x))))


def _silu(x):
    return x / (1.0 + jnp.exp(-x))


def _log_sigmoid(x):
    return jnp.minimum(x, 0.0) - jnp.log(1.0 + jnp.exp(-jnp.abs(x)))


def _layer_norm_rows(h, g, b):
    mu = jnp.mean(h, axis=-1, keepdims=True)
    hc = h - mu
    var = jnp.mean(hc * hc, axis=-1, keepdims=True)
    return hc * lax.rsqrt(var + LN_EPS) * g + b


def _layer_a_kernel(x_ref, wu_ref, wv_ref, wz_ref, wo_ref, ws_ref, bs_ref, vg_ref, vb_ref,
                    lng_ref, lnb_ref, o_ref,
                    xb_ref, gv_ref, sum_ref, sq_ref, acc_ref, y_ref, *, nc, d_inner, alpha):
    j = pl.program_id(1)
    tm = x_ref.shape[0]
    cw = wv_ref.shape[1]
    gdim = d_inner // N_GROUPS_A

    @pl.when(j == 0)
    def _init():
        xb_ref[...] = x_ref[...].astype(BF16)
        sum_ref[...] = jnp.zeros_like(sum_ref)
        sq_ref[...] = jnp.zeros_like(sq_ref)
        acc_ref[...] = jnp.zeros_like(acc_ref)

    @pl.when(j < nc)
    def _phase_v():
        gv = _gelu_tanh(_dot(xb_ref[...], wv_ref[...]))
        gv_ref[j] = gv
        sum_ref[...] += jnp.sum(gv, axis=1, keepdims=True)
        sq_ref[...] += jnp.sum(gv * gv, axis=1, keepdims=True)

    @pl.when(j >= nc)
    def _phase_mix():
        c = j - nc
        mu = sum_ref[...] * (1.0 / d_inner)
        var = sq_ref[...] * (1.0 / d_inner) - mu * mu
        rstd = lax.rsqrt(var + LN_EPS)
        vn = ((gv_ref[c] - mu) * rstd * vg_ref[...] + vb_ref[...]).astype(BF16)
        xb = xb_ref[...]
        u = _gelu_tanh(_dot(xb, wu_ref[...]))
        z = _dot(xb, wz_ref[...])
        gate = u * _silu(z)
        for g in range(cw // gdim):
            cols = slice(g * gdim, (g + 1) * gdim)
            wsg = ws_ref[g]
            bsg = bs_ref[g]
            for n in range(tm // CHUNK_A):
                rows = slice(n * CHUNK_A, (n + 1) * CHUNK_A)
                s = _dot(wsg, vn[rows, cols]) + bsg
                y_ref[rows, cols] = (gate[rows, cols] * s).astype(BF16)
        acc_ref[...] += _dot(y_ref[...], wo_ref[...])

    @pl.when(j == 2 * nc - 1)
    def _finish():
        h = alpha * x_ref[...] + acc_ref[...]
        o_ref[...] = _layer_norm_rows(h, lng_ref[...], lnb_ref[...])


def _layer_a(x2, w_in, w_s, b_s, w_out, vg, vb, lng, lnb, alpha):
    rows, d = x2.shape
    d_inner = w_out.shape[0]
    nc = d_inner // CW_A
    gpc = CW_A // (d_inner // N_GROUPS_A)
    tm = TM_A
    grid = (rows // tm, 2 * nc)

    def cmix(j):
        return jnp.maximum(j - nc, 0)

    kernel = functools.partial(_layer_a_kernel, nc=nc, d_inner=d_inner, alpha=alpha)
    return pl.pallas_call(
        kernel,
        grid=grid,
        in_specs=[
            pl.BlockSpec((tm, d), lambda i, j: (i, 0)),
            pl.BlockSpec((d, CW_A), lambda i, j: (0, cmix(j))),
            pl.BlockSpec((d, CW_A), lambda i, j: (0, nc + jnp.minimum(j, nc - 1))),
            pl.BlockSpec((d, CW_A), lambda i, j: (0, 2 * nc + cmix(j))),
            pl.BlockSpec((CW_A, d), lambda i, j: (cmix(j), 0)),
            pl.BlockSpec((gpc, CHUNK_A, CHUNK_A), lambda i, j: (cmix(j), 0, 0)),
            pl.BlockSpec((gpc, CHUNK_A, 1), lambda i, j: (cmix(j), 0, 0)),
            pl.BlockSpec((1, CW_A), lambda i, j: (0, cmix(j))),
            pl.BlockSpec((1, CW_A), lambda i, j: (0, cmix(j))),
            pl.BlockSpec((1, d), lambda i, j: (0, 0)),
            pl.BlockSpec((1, d), lambda i, j: (0, 0)),
        ],
        out_specs=pl.BlockSpec((tm, d), lambda i, j: (i, 0)),
        out_shape=jax.ShapeDtypeStruct((rows, d), F32),
        scratch_shapes=[
            pltpu.VMEM((tm, d), BF16),
            pltpu.VMEM((nc, tm, CW_A), F32),
            pltpu.VMEM((tm, 1), F32),
            pltpu.VMEM((tm, 1), F32),
            pltpu.VMEM((tm, d), F32),
            pltpu.VMEM((tm, CW_A), BF16),
        ],
        compiler_params=pltpu.CompilerParams(
            dimension_semantics=("arbitrary", "arbitrary"),
            vmem_limit_bytes=VMEM_LIMIT),
        name="layer_a",
    )(x2, w_in, w_in, w_in, w_out, w_s, b_s, vg, vb, lng, lnb)


def _proj_b_kernel(x_ref, w_ref, wgl_ref, wg_ref, bg_ref, lf_ref, lb_ref,
                   h_ref, gf_ref, rb_ref, xb_ref, *, key_dim):
    j = pl.program_id(1)
    tm = x_ref.shape[0]

    @pl.when(j == 0)
    def _gates():
        xb = x_ref[...].astype(BF16)
        xb_ref[...] = xb
        gl = _dot(xb, wgl_ref[...])
        pre = _dot(gl.astype(BF16), wg_ref[...]) + bg_ref[...]
        glog = _log_sigmoid(pre) * (1.0 / GATE_TAU)
        for r in range(tm // CUM_ROWS):
            rows = slice(r * CUM_ROWS, (r + 1) * CUM_ROWS)
            for dst, tri, cols in ((gf_ref, lf_ref, slice(0, key_dim)),
                                   (rb_ref, lb_ref, slice(key_dim, 2 * key_dim))):
                gb = glog[rows, cols]
                hi = gb.astype(BF16)
                lo = (gb - hi.astype(F32)).astype(BF16)
                dst[rows, :] = _dot(tri[...], hi) + _dot(tri[...], lo)

    h_ref[...] = _dot(xb_ref[...], w_ref[...]).astype(BF16)


def _proj_b(x2, w_main, w_gl, w_gate, b_gate, tri_f, tri_b):
    rows, d = x2.shape
    n_main = w_main.shape[1]
    key_dim = w_gate.shape[1] // 2
    tm = TM_P
    grid = (rows // tm, n_main // CW_P)
    kernel = functools.partial(_proj_b_kernel, key_dim=key_dim)
    return pl.pallas_call(
        kernel,
        grid=grid,
        in_specs=[
            pl.BlockSpec((tm, d), lambda i, j: (i, 0)),
            pl.BlockSpec((d, CW_P), lambda i, j: (0, j)),
            pl.BlockSpec(w_gl.shape, lambda i, j: (0, 0)),
            pl.BlockSpec(w_gate.shape, lambda i, j: (0, 0)),
            pl.BlockSpec(b_gate.shape, lambda i, j: (0, 0)),
            pl.BlockSpec(tri_f.shape, lambda i, j: (0, 0)),
            pl.BlockSpec(tri_b.shape, lambda i, j: (0, 0)),
        ],
        out_specs=[
            pl.BlockSpec((tm, CW_P), lambda i, j: (i, j)),
            pl.BlockSpec((tm, key_dim), lambda i, j: (i, 0)),
            pl.BlockSpec((tm, key_dim), lambda i, j: (i, 0)),
        ],
        out_shape=[
            jax.ShapeDtypeStruct((rows, n_main), BF16),
            jax.ShapeDtypeStruct((rows, key_dim), F32),
            jax.ShapeDtypeStruct((rows, key_dim), F32),
        ],
        scratch_shapes=[pltpu.VMEM((tm, d), BF16)],
        compiler_params=pltpu.CompilerParams(
            dimension_semantics=("arbitrary", "arbitrary"),
            vmem_limit_bytes=VMEM_LIMIT),
        name="proj_b",
    )(x2, w_main, w_gl, w_gate, b_gate, tri_f, tri_b)


def _row_to_col(vec_row, width):
    n = vec_row.shape[1]
    return jnp.broadcast_to(vec_row, (width, n)).T


def _gla_macro_chunk(q, k, v, cum, state, reverse):
    c_rows, dk = q.shape
    ns = c_rows // SUB_B
    qs = q.astype(F32) * (dk ** -0.5)
    kf = k.astype(F32)
    order = list(range(ns))[::-1] if reverse else list(range(ns))
    mid_row = SUB_B // 2 - 1 if reverse else SUB_B // 2
    end_row = 0 if reverse else SUB_B - 1

    rsl = {a: slice(a * SUB_B, (a + 1) * SUB_B) for a in range(ns)}
    r_end, q_diag, k_diag, q_off, k_off = {}, {}, {}, {}, {}
    r_prev = {}
    zero_row = jnp.zeros((1, dk), F32)
    prev = zero_row
    for a in order:
        ca = cum[rsl[a]]
        mid = ca[mid_row:mid_row + 1]
        r_end[a] = ca[end_row:end_row + 1]
        r_prev[a] = prev
        q_diag[a] = (qs[rsl[a]] * jnp.exp(ca - mid)).astype(BF16)
        k_diag[a] = (kf[rsl[a]] * jnp.exp(mid - ca)).astype(BF16)
        q_off[a] = qs[rsl[a]] * jnp.exp(ca - prev)
        k_off[a] = kf[rsl[a]] * jnp.exp(r_end[a] - ca)
        prev = r_end[a]
    r_last = prev

    ri = lax.broadcasted_iota(jnp.int32, (SUB_B, SUB_B), 0)
    ci = lax.broadcasted_iota(jnp.int32, (SUB_B, SUB_B), 1)
    keep = (ci > ri) if reverse else (ci <= ri)

    k_off_b = {a: k_off[a].astype(BF16) for a in range(ns)}
    a_rows = []
    q_state = []
    for a in range(ns):
        blocks = []
        for b in range(ns):
            if b == a:
                blk = jnp.where(keep, _dot_nt(q_diag[a], k_diag[a]), 0.0)
            elif order.index(b) < order.index(a):
                if order.index(b) + 1 == order.index(a):
                    qa = q_off[a]
                else:
                    qa = q_off[a] * jnp.exp(r_prev[a] - r_end[b])
                blk = _dot_nt(qa.astype(BF16), k_off_b[b])
            else:
                blk = jnp.zeros((SUB_B, SUB_B), F32)
            blocks.append(blk)
        a_rows.append(jnp.concatenate(blocks, axis=1) if ns > 1 else blocks[0])
        q_state.append(q_off[a] * jnp.exp(r_prev[a]))
    a_full = jnp.concatenate(a_rows, axis=0) if ns > 1 else a_rows[0]
    q_st = jnp.concatenate(q_state, axis=0) if ns > 1 else q_state[0]

    o = _dot(a_full.astype(BF16), v) + _dot(q_st.astype(BF16), state.astype(BF16))

    k_dec = [k_off[a] * jnp.exp(r_last - r_end[a]) for a in range(ns)]
    k_dec = (jnp.concatenate(k_dec, axis=0) if ns > 1 else k_dec[0]).astype(BF16)
    upd = _dot_tn(k_dec, v)
    dcol = _row_to_col(jnp.exp(r_last), 128)
    dv = v.shape[1]
    new_state = jnp.concatenate(
        [state[:, n * 128:(n + 1) * 128] * dcol for n in range(dv // 128)], axis=1) + upd
    return o, new_state


def _gla_sweep_tile(q_ref, k_ref, v_ref, cum_ref, s_ref, h, emit, reverse):
    tb = q_ref.shape[0]
    nm = tb // MACRO_B
    chunks = range(nm - 1, -1, -1) if reverse else range(nm)
    for m in chunks:
        rows = slice(m * MACRO_B, (m + 1) * MACRO_B)
        o, s_new = _gla_macro_chunk(q_ref[rows, :], k_ref[rows, :], v_ref[rows, :], cum_ref[rows, :],
                                    s_ref[h], reverse)
        s_ref[h] = s_new
        emit(rows, o)


def _gla_fwd_kernel(q_ref, k_ref, v_ref, cum_ref, o_ref, s_ref):
    i = pl.program_id(1)
    h = pl.program_id(2)

    @pl.when(i == 0)
    def _reset():
        s_ref[h] = jnp.zeros(s_ref.shape[1:], F32)

    def emit(rows, o):
        o_ref[rows, :] = o

    _gla_sweep_tile(q_ref, k_ref, v_ref, cum_ref, s_ref, h, emit, reverse=False)


def _gla_fwd(hmain, cum_f, batch, key_dim, d_inner):
    rows = hmain.shape[0]
    dk = key_dim // GLA_HEADS
    dv = d_inner // GLA_HEADS
    tb = TB_G
    nt = rows // batch // tb
    grid = (batch, nt, GLA_HEADS)
    kq, kv = key_dim // dk, key_dim * 2 // dv
    return pl.pallas_call(
        _gla_fwd_kernel,
        grid=grid,
        in_specs=[
            pl.BlockSpec((tb, dk), lambda b, i, h: (b * nt + i, h)),
            pl.BlockSpec((tb, dk), lambda b, i, h: (b * nt + i, kq + h)),
            pl.BlockSpec((tb, dv), lambda b, i, h: (b * nt + i, kv + h)),
            pl.BlockSpec((tb, dk), lambda b, i, h: (b * nt + i, h)),
        ],
        out_specs=pl.BlockSpec((tb, dv), lambda b, i, h: (b * nt + i, h)),
        out_shape=jax.ShapeDtypeStruct((rows, d_inner), F32),
        scratch_shapes=[pltpu.VMEM((GLA_HEADS, dk, dv), F32)],
        compiler_params=pltpu.CompilerParams(
            dimension_semantics=("arbitrary", "arbitrary", "arbitrary"),
            vmem_limit_bytes=VMEM_LIMIT),
        name="gla_fwd",
    )(hmain, hmain, hmain, cum_f)


def _gla_bwd_kernel(q_ref, k_ref, v_ref, cum_ref, of_ref, z_ref, x_ref, wo_ref, gn_ref,
                    lng_ref, lnb_ref, o_ref, s_ref, y_ref, acc_ref, *, alpha):
    i = pl.program_id(1)
    h = pl.program_id(2)

    @pl.when(i == 0)
    def _reset():
        s_ref[h] = jnp.zeros(s_ref.shape[1:], F32)

    @pl.when(h == 0)
    def _zero_acc():
        acc_ref[...] = jnp.zeros_like(acc_ref)

    def emit(rows, o_b):
        o = of_ref[rows, :] + o_b
        o = o * lax.rsqrt(jnp.mean(o * o, axis=-1, keepdims=True) + RMS_EPS) * gn_ref[...]
        y_ref[rows, :] = (o * _silu(z_ref[rows, :].astype(F32))).astype(BF16)

    _gla_sweep_tile(q_ref, k_ref, v_ref, cum_ref, s_ref, h, emit, reverse=True)
    acc_ref[...] += _dot(y_ref[...], wo_ref[...])

    @pl.when(h == GLA_HEADS - 1)
    def _finish():
        hres = alpha * x_ref[...] + acc_ref[...]
        o_ref[...] = _layer_norm_rows(hres, lng_ref[...], lnb_ref[...])


def _gla_bwd(hmain, cum_b, o_f, x2, w_out, gn, lng, lnb, batch, key_dim, alpha):
    rows, d = x2.shape
    d_inner = w_out.shape[0]
    dk = key_dim // GLA_HEADS
    dv = d_inner // GLA_HEADS
    tb = TB_G
    nt = rows // batch // tb
    grid = (batch, nt, GLA_HEADS)
    kq, kv = key_dim // dk, key_dim * 2 // dv
    kz = (2 * key_dim + d_inner) // dv

    def row(b, i):
        return b * nt + (nt - 1 - i)

    kernel = functools.partial(_gla_bwd_kernel, alpha=alpha)
    return pl.pallas_call(
        kernel,
        grid=grid,
        in_specs=[
            pl.BlockSpec((tb, dk), lambda b, i, h: (row(b, i), h)),
            pl.BlockSpec((tb, dk), lambda b, i, h: (row(b, i), kq + h)),
            pl.BlockSpec((tb, dv), lambda b, i, h: (row(b, i), kv + h)),
            pl.BlockSpec((tb, dk), lambda b, i, h: (row(b, i), h)),
            pl.BlockSpec((tb, dv), lambda b, i, h: (row(b, i), h)),
            pl.BlockSpec((tb, dv), lambda b, i, h: (row(b, i), kz + h)),
            pl.BlockSpec((tb, d), lambda b, i, h: (row(b, i), 0)),
            pl.BlockSpec((dv, d), lambda b, i, h: (h, 0)),
            pl.BlockSpec((1, dv), lambda b, i, h: (0, h)),
            pl.BlockSpec((1, d), lambda b, i, h: (0, 0)),
            pl.BlockSpec((1, d), lambda b, i, h: (0, 0)),
        ],
        out_specs=pl.BlockSpec((tb, d), lambda b, i, h: (row(b, i), 0)),
        out_shape=jax.ShapeDtypeStruct((rows, d), F32),
        scratch_shapes=[
            pltpu.VMEM((GLA_HEADS, dk, dv), F32),
            pltpu.VMEM((tb, dv), BF16),
            pltpu.VMEM((tb, d), F32),
        ],
        compiler_params=pltpu.CompilerParams(
            dimension_semantics=("arbitrary", "arbitrary", "arbitrary"),
            vmem_limit_bytes=VMEM_LIMIT),
        name="gla_bwd",
    )(hmain, hmain, hmain, cum_b, o_f, hmain, x2, w_out, gn, lng, lnb)


def _tri_blocks(reverse):
    r = jnp.arange(CUM_ROWS)[:, None]
    c = jnp.arange(CUM_ROWS)[None, :]
    same = (r // MACRO_B) == (c // MACRO_B)
    tri = (c >= r) if reverse else (c <= r)
    return (same & tri).astype(BF16)


def _prep_a(w_in, vg, vb, w_s, b_s, w_out):
    return dict(w_in=w_in.astype(BF16), w_s=w_s.astype(BF16), b_s=b_s[:, :, None].astype(F32),
                w_out=w_out.astype(BF16), vg=vg[None, :], vb=vb[None, :])


def _prep_b(w_in, w_g2, b_g, gn_g, w_out):
    key_dim = w_g2.shape[2]
    d_inner = w_out.shape[0]
    n_main = 2 * key_dim + 2 * d_inner
    zeros = jnp.zeros((GATE_RANK, key_dim), F32)
    w_gate = jnp.concatenate([jnp.concatenate([w_g2[0], zeros], axis=1),
                              jnp.concatenate([zeros, w_g2[1]], axis=1)], axis=0)
    return dict(w_main=w_in[:, :n_main].astype(BF16), w_gl=w_in[:, n_main:].astype(BF16),
                w_gate=w_gate.astype(BF16), b_gate=b_g.reshape(1, 2 * key_dim),
                gn=gn_g[None, :], w_out=w_out.astype(BF16), key_dim=key_dim, d_inner=d_inner,
                tri_f=_tri_blocks(False), tri_b=_tri_blocks(True))


def _mixer_a_layer(x2, p, lng, lnb, alpha):
    return _layer_a(x2, p["w_in"], p["w_s"], p["b_s"], p["w_out"], p["vg"], p["vb"], lng, lnb, alpha)


def _mixer_b_layer(x2, batch, p, lng, lnb, alpha):
    hmain, cum_f, cum_b = _proj_b(x2, p["w_main"], p["w_gl"], p["w_gate"], p["b_gate"],
                                  p["tri_f"], p["tri_b"])
    o_f = _gla_fwd(hmain, cum_f, batch, p["key_dim"], p["d_inner"])
    return _gla_bwd(hmain, cum_b, o_f, x2, p["w_out"], p["gn"], lng, lnb, batch, p["key_dim"], alpha)


def _trunk(x, layers, ln_g, ln_b, alpha):
    batch, seq, d = x.shape
    x2 = x.reshape(batch * seq, d)
    for i, (kind, p) in enumerate(layers):
        lng, lnb = ln_g[i][None, :], ln_b[i][None, :]
        if kind == "a":
            x2 = _mixer_a_layer(x2, p, lng, lnb, alpha)
        else:
            x2 = _mixer_b_layer(x2, batch, p, lng, lnb, alpha)
    return x2.reshape(batch, seq, d)


def kernel(x_prompt, x_sample, w_in_a, ln_v_g_a, ln_v_b_a, w_s_a, b_s_a, w_out_a, w_in_b, w_g2_b, b_g_b, gn_g_b, w_out_b, ln_g, ln_b):
    depth = ln_g.shape[0]
    alpha = (2 * depth) ** 0.25
    layers = []
    for i in range(depth):
        j = i // 2
        if i % 2 == 0:
            layers.append(("a", _prep_a(w_in_a[j], ln_v_g_a[j], ln_v_b_a[j], w_s_a[j], b_s_a[j], w_out_a[j])))
        else:
            layers.append(("b", _prep_b(w_in_b[j], w_g2_b[j], b_g_b[j], gn_g_b[j], w_out_b[j])))
    y_prompt = _trunk(x_prompt, layers, ln_g, ln_b, alpha)
    y_sample = _trunk(x_sample, layers, ln_g, ln_b, alpha)
    return (y_prompt, y_sample)
```

```python
import functools

import jax
import jax.numpy as jnp
from jax import lax
from jax.experimental import pallas as pl
from jax.experimental.pallas import tpu as pltpu

F32 = jnp.float32
BF16 = jnp.bfloat16

CHUNK_A = 128
N_GROUPS_A = 16
GLA_HEADS = 4
GATE_RANK = 16
GATE_TAU = 16.0
SUB_B = 64
LN_EPS = 1e-5
RMS_EPS = 1e-6

TM_A = 1024
CW_A = 512
RB_A = 256
TM_P = 1024
CW_P = 1024
TB_G = 1024
MACRO_B = 128
CUM_ROWS = 256
VMEM_LIMIT = 60 * 1024 * 1024


def _dot(a, b):
    return jnp.dot(a, b, preferred_element_type=F32)


def _dot_nt(a, b):
    return lax.dot_general(a, b, (((1,), (1,)), ((), ())), preferred_element_type=F32)


def _dot_tn(a, b):
    return lax.dot_general(a, b, (((0,), (0,)), ((), ())), preferred_element_type=F32)


def _gelu_tanh(x):
    c = 0.7978845608028654
    return 0.5 * x * (1.0 + jnp.tanh(c * (x + 0.044715 * (x * x * x))))


def _silu(x):
    return x / (1.0 + jnp.exp(-x))


def _log_sigmoid(x):
    return jnp.minimum(x, 0.0) - jnp.log(1.0 + jnp.exp(-jnp.abs(x)))


def _layer_norm_rows(h, g, b):
    mu = jnp.mean(h, axis=-1, keepdims=True)
    hc = h - mu
    var = jnp.mean(hc * hc, axis=-1, keepdims=True)
    return hc * lax.rsqrt(var + LN_EPS) * g + b


def _chunked_cols(w, cw):
    k, n = w.shape
    return w.reshape(k, n // cw, cw).transpose(1, 0, 2)


def _layer_a_kernel(xb_ref, x_hbm, wu_ref, wv_ref, wz_ref, wo_ref, ws_ref, bs_ref, vg_ref, vb_ref,
                    lng_ref, lnb_ref, o_ref,
                    gv_ref, sum_ref, sq_ref, y_ref, sem, *, nc, d_inner, alpha):
    i = pl.program_id(0)
    j = pl.program_id(1)
    tm = o_ref.shape[0]
    cw = wv_ref.shape[1]
    gdim = d_inner // N_GROUPS_A

    def x_copy():
        return pltpu.make_async_copy(x_hbm.at[pl.ds(i * tm, tm), :], o_ref, sem)

    @pl.when(j == 0)
    def _init():
        x_copy().start()
        sum_ref[...] = jnp.zeros_like(sum_ref)
        sq_ref[...] = jnp.zeros_like(sq_ref)

    @pl.when(j < nc)
    def _phase_v():
        for r in range(tm // RB_A):
            rows = slice(r * RB_A, (r + 1) * RB_A)
            gv = _gelu_tanh(_dot(xb_ref[rows, :], wv_ref[...]))
            gv_ref[j, rows, :] = gv.astype(BF16)
            sum_ref[rows, :] += jnp.sum(gv, axis=1, keepdims=True)
            sq_ref[rows, :] += jnp.sum(gv * gv, axis=1, keepdims=True)

    @pl.when(j == nc)
    def _start_acc():
        x_copy().wait()
        o_ref[...] = alpha * o_ref[...]

    @pl.when(j >= nc)
    def _phase_mix():
        c = j - nc
        for r in range(tm // RB_A):
            rows = slice(r * RB_A, (r + 1) * RB_A)
            mu = sum_ref[rows, :] * (1.0 / d_inner)
            var = sq_ref[rows, :] * (1.0 / d_inner) - mu * mu
            rstd = lax.rsqrt(var + LN_EPS)
            vn = ((gv_ref[c, rows, :].astype(F32) - mu) * rstd * vg_ref[...] + vb_ref[...]).astype(BF16)
            xb = xb_ref[rows, :]
            u = _gelu_tanh(_dot(xb, wu_ref[...]))
            z = _dot(xb, wz_ref[...])
            gate = u * _silu(z)
            for g in range(cw // gdim):
                cols = slice(g * gdim, (g + 1) * gdim)
                wsg = ws_ref[g]
                bsg = bs_ref[g]
                for n in range(RB_A // CHUNK_A):
                    crows = slice(n * CHUNK_A, (n + 1) * CHUNK_A)
                    s = _dot(wsg, vn[crows, cols]) + bsg
                    y_ref[r * RB_A + n * CHUNK_A:r * RB_A + (n + 1) * CHUNK_A, cols] = (
                        gate[crows, cols] * s).astype(BF16)
            o_ref[rows, :] += _dot(y_ref[rows, :], wo_ref[...])

    @pl.when(j == 2 * nc - 1)
    def _finish():
        for r in range(tm // RB_A):
            rows = slice(r * RB_A, (r + 1) * RB_A)
            o_ref[rows, :] = _layer_norm_rows(o_ref[rows, :], lng_ref[...], lnb_ref[...])


def _layer_a(x2, p, lng, lnb, alpha):
    rows, d = x2.shape
    w_in, w_out = p["w_in"], p["w_out"]
    d_inner = w_out.shape[0]
    cw = w_in.shape[2]
    nc = d_inner // cw
    gpc = cw // (d_inner // N_GROUPS_A)
    tm = min(TM_A, rows)
    grid = (rows // tm, 2 * nc)
    xb = x2.astype(BF16)

    def cmix(j):
        return jnp.maximum(j - nc, 0)

    kernel = functools.partial(_layer_a_kernel, nc=nc, d_inner=d_inner, alpha=alpha)
    return pl.pallas_call(
        kernel,
        grid=grid,
        in_specs=[
            pl.BlockSpec((tm, d), lambda i, j: (i, 0)),
            pl.BlockSpec(memory_space=pl.ANY),
            pl.BlockSpec((None, d, cw), lambda i, j: (cmix(j), 0, 0)),
            pl.BlockSpec((None, d, cw), lambda i, j: (nc + jnp.minimum(j, nc - 1), 0, 0)),
            pl.BlockSpec((None, d, cw), lambda i, j: (2 * nc + cmix(j), 0, 0)),
            pl.BlockSpec((cw, d), lambda i, j: (cmix(j), 0)),
            pl.BlockSpec((gpc, CHUNK_A, CHUNK_A), lambda i, j: (cmix(j), 0, 0)),
            pl.BlockSpec((gpc, CHUNK_A, 1), lambda i, j: (cmix(j), 0, 0)),
            pl.BlockSpec((1, cw), lambda i, j: (0, cmix(j))),
            pl.BlockSpec((1, cw), lambda i, j: (0, cmix(j))),
            pl.BlockSpec((1, d), lambda i, j: (0, 0)),
            pl.BlockSpec((1, d), lambda i, j: (0, 0)),
        ],
        out_specs=pl.BlockSpec((tm, d), lambda i, j: (i, 0)),
        out_shape=jax.ShapeDtypeStruct((rows, d), F32),
        scratch_shapes=[
            pltpu.VMEM((nc, tm, cw), BF16),
            pltpu.VMEM((tm, 1), F32),
            pltpu.VMEM((tm, 1), F32),
            pltpu.VMEM((tm, cw), BF16),
            pltpu.SemaphoreType.DMA,
        ],
        compiler_params=pltpu.CompilerParams(
            dimension_semantics=("arbitrary", "arbitrary"),
            vmem_limit_bytes=VMEM_LIMIT),
        name="layer_a",
    )(xb, x2, w_in, w_in, w_in, w_out, p["w_s"], p["b_s"], p["vg"], p["vb"], lng, lnb)


def _proj_b_kernel(x_ref, w_ref, wgl_ref, wg_ref, bg_ref, lf_ref, lb_ref,
                   h_ref, gf_ref, rb_ref, xb_ref, *, key_dim):
    j = pl.program_id(1)
    tm = x_ref.shape[0]

    @pl.when(j == 0)
    def _gates():
        for r in range(tm // CUM_ROWS):
            rows = slice(r * CUM_ROWS, (r + 1) * CUM_ROWS)
            xb = x_ref[rows, :].astype(BF16)
            xb_ref[rows, :] = xb
            gl = _dot(xb, wgl_ref[...])
            pre = _dot(gl.astype(BF16), wg_ref[...]) + bg_ref[...]
            glog = _log_sigmoid(pre) * (1.0 / GATE_TAU)
            for dst, tri, cols in ((gf_ref, lf_ref, slice(0, key_dim)),
                                   (rb_ref, lb_ref, slice(key_dim, 2 * key_dim))):
                gb = glog[:, cols]
                hi = gb.astype(BF16)
                lo = (gb - hi.astype(F32)).astype(BF16)
                dst[rows, :] = _dot(tri[...], hi) + _dot(tri[...], lo)

    h_ref[...] = _dot(xb_ref[...], w_ref[...]).astype(BF16)


def _proj_b(x2, p):
    rows, d = x2.shape
    w_main = p["w_main"]
    n_chunks, _, cw = w_main.shape
    key_dim = p["key_dim"]
    tm = min(TM_P, rows)
    grid = (rows // tm, n_chunks)
    kernel = functools.partial(_proj_b_kernel, key_dim=key_dim)
    small = [p["w_gl"], p["w_gate"], p["b_gate"], p["tri_f"], p["tri_b"]]
    return pl.pallas_call(
        kernel,
        grid=grid,
        in_specs=[
            pl.BlockSpec((tm, d), lambda i, j: (i, 0)),
            pl.BlockSpec((None, d, cw), lambda i, j: (j, 0, 0)),
        ] + [pl.BlockSpec(a.shape, lambda i, j: (0, 0)) for a in small],
        out_specs=[
            pl.BlockSpec((tm, cw), lambda i, j: (i, j)),
            pl.BlockSpec((tm, key_dim), lambda i, j: (i, 0)),
            pl.BlockSpec((tm, key_dim), lambda i, j: (i, 0)),
        ],
        out_shape=[
            jax.ShapeDtypeStruct((rows, n_chunks * cw), BF16),
            jax.ShapeDtypeStruct((rows, key_dim), F32),
            jax.ShapeDtypeStruct((rows, key_dim), F32),
        ],
        scratch_shapes=[pltpu.VMEM((tm, d), BF16)],
        compiler_params=pltpu.CompilerParams(
            dimension_semantics=("arbitrary", "arbitrary"),
            vmem_limit_bytes=VMEM_LIMIT),
        name="proj_b",
    )(x2, w_main, *small)


def _row_to_col(vec_row, width):
    n = vec_row.shape[1]
    return jnp.broadcast_to(vec_row, (width, n)).T


def _gla_macro_chunk(q, k, v, cum, state, reverse):
    c_rows, dk = q.shape
    ns = c_rows // SUB_B
    qs = q.astype(F32) * (dk ** -0.5)
    kf = k.astype(F32)
    order = list(range(ns))[::-1] if reverse else list(range(ns))
    mid_row = SUB_B // 2 - 1 if reverse else SUB_B // 2
    end_row = 0 if reverse else SUB_B - 1

    rsl = {a: slice(a * SUB_B, (a + 1) * SUB_B) for a in range(ns)}
    r_end, q_diag, k_diag, q_off, k_off = {}, {}, {}, {}, {}
    r_prev = {}
    zero_row = jnp.zeros((1, dk), F32)
    prev = zero_row
    for a in order:
        ca = cum[rsl[a]]
        mid = ca[mid_row:mid_row + 1]
        r_end[a] = ca[end_row:end_row + 1]
        r_prev[a] = prev
        q_diag[a] = (qs[rsl[a]] * jnp.exp(ca - mid)).astype(BF16)
        k_diag[a] = (kf[rsl[a]] * jnp.exp(mid - ca)).astype(BF16)
        q_off[a] = qs[rsl[a]] * jnp.exp(ca - prev)
        k_off[a] = kf[rsl[a]] * jnp.exp(r_end[a] - ca)
        prev = r_end[a]
    r_last = prev

    ri = lax.broadcasted_iota(jnp.int32, (SUB_B, SUB_B), 0)
    ci = lax.broadcasted_iota(jnp.int32, (SUB_B, SUB_B), 1)
    keep = (ci > ri) if reverse else (ci <= ri)

    k_off_b = {a: k_off[a].astype(BF16) for a in range(ns)}
    a_rows = []
    q_state = []
    for a in range(ns):
        blocks = []
        for b in range(ns):
            if b == a:
                blk = jnp.where(keep, _dot_nt(q_diag[a], k_diag[a]), 0.0)
            elif order.index(b) < order.index(a):
                if order.index(b) + 1 == order.index(a):
                    qa = q_off[a]
                else:
                    qa = q_off[a] * jnp.exp(r_prev[a] - r_end[b])
                blk = _dot_nt(qa.astype(BF16), k_off_b[b])
            else:
                blk = jnp.zeros((SUB_B, SUB_B), F32)
            blocks.append(blk)
        a_rows.append(jnp.concatenate(blocks, axis=1) if ns > 1 else blocks[0])
        q_state.append(q_off[a] * jnp.exp(r_prev[a]))
    a_full = jnp.concatenate(a_rows, axis=0) if ns > 1 else a_rows[0]
    q_st = jnp.concatenate(q_state, axis=0) if ns > 1 else q_state[0]

    o = _dot(a_full.astype(BF16), v) + _dot(q_st.astype(BF16), state.astype(BF16))

    k_dec = [k_off[a] * jnp.exp(r_last - r_end[a]) for a in range(ns)]
    k_dec = (jnp.concatenate(k_dec, axis=0) if ns > 1 else k_dec[0]).astype(BF16)
    upd = _dot_tn(k_dec, v)
    dcol = _row_to_col(jnp.exp(r_last), 128)
    dv = v.shape[1]
    new_state = jnp.concatenate(
        [state[:, n * 128:(n + 1) * 128] * dcol for n in range(dv // 128)], axis=1) + upd
    return o, new_state


def _gla_sweep_tile(q_ref, k_ref, v_ref, cum_ref, s_ref, h, emit, reverse):
    tb = q_ref.shape[0]
    nm = tb // MACRO_B
    chunks = range(nm - 1, -1, -1) if reverse else range(nm)
    for m in chunks:
        rows = slice(m * MACRO_B, (m + 1) * MACRO_B)
        o, s_new = _gla_macro_chunk(q_ref[rows, :], k_ref[rows, :], v_ref[rows, :], cum_ref[rows, :],
                                    s_ref[h], reverse)
        s_ref[h] = s_new
        emit(rows, o)


def _gla_fwd_kernel(q_ref, k_ref, v_ref, cum_ref, o_ref, s_ref):
    i = pl.program_id(1)
    h = pl.program_id(2)

    @pl.when(i == 0)
    def _reset():
        s_ref[h] = jnp.zeros(s_ref.shape[1:], F32)

    def emit(rows, o):
        o_ref[rows, :] = o.astype(o_ref.dtype)

    _gla_sweep_tile(q_ref, k_ref, v_ref, cum_ref, s_ref, h, emit, reverse=False)


def _gla_fwd(hmain, cum_f, batch, key_dim, d_inner):
    rows = hmain.shape[0]
    dk = key_dim // GLA_HEADS
    dv = d_inner // GLA_HEADS
    tb = min(TB_G, rows // batch)
    nt = rows // batch // tb
    grid = (batch, nt, GLA_HEADS)
    kq, kv = key_dim // dk, key_dim * 2 // dv
    return pl.pallas_call(
        _gla_fwd_kernel,
        grid=grid,
        in_specs=[
            pl.BlockSpec((tb, dk), lambda b, i, h: (b * nt + i, h)),
            pl.BlockSpec((tb, dk), lambda b, i, h: (b * nt + i, kq + h)),
            pl.BlockSpec((tb, dv), lambda b, i, h: (b * nt + i, kv + h)),
            pl.BlockSpec((tb, dk), lambda b, i, h: (b * nt + i, h)),
        ],
        out_specs=pl.BlockSpec((tb, dv), lambda b, i, h: (b * nt + i, h)),
        out_shape=jax.ShapeDtypeStruct((rows, d_inner), BF16),
        scratch_shapes=[pltpu.VMEM((GLA_HEADS, dk, dv), F32)],
        compiler_params=pltpu.CompilerParams(
            dimension_semantics=("arbitrary", "arbitrary", "arbitrary"),
            vmem_limit_bytes=VMEM_LIMIT),
        name="gla_fwd",
    )(hmain, hmain, hmain, cum_f)


def _gla_bwd_kernel(q_ref, k_ref, v_ref, cum_ref, of_ref, z_ref, x_hbm, wo_ref, gn_ref,
                    lng_ref, lnb_ref, o_ref, s_ref, y_ref, sem, *, alpha, nt):
    b = pl.program_id(0)
    i = pl.program_id(1)
    h = pl.program_id(2)
    tb = o_ref.shape[0]

    def x_copy():
        row0 = (b * nt + (nt - 1 - i)) * tb
        return pltpu.make_async_copy(x_hbm.at[pl.ds(row0, tb), :], o_ref, sem)

    @pl.when(i == 0)
    def _reset():
        s_ref[h] = jnp.zeros(s_ref.shape[1:], F32)

    @pl.when(h == 0)
    def _fetch_x():
        x_copy().start()

    def emit(rows, o_b):
        o = of_ref[rows, :].astype(F32) + o_b
        o = o * lax.rsqrt(jnp.mean(o * o, axis=-1, keepdims=True) + RMS_EPS) * gn_ref[...]
        y_ref[rows, :] = (o * _silu(z_ref[rows, :].astype(F32))).astype(BF16)

    _gla_sweep_tile(q_ref, k_ref, v_ref, cum_ref, s_ref, h, emit, reverse=True)

    @pl.when(h == 0)
    def _start_acc():
        x_copy().wait()
        o_ref[...] = alpha * o_ref[...]

    for r in range(tb // CUM_ROWS):
        rows = slice(r * CUM_ROWS, (r + 1) * CUM_ROWS)
        o_ref[rows, :] += _dot(y_ref[rows, :], wo_ref[...])

    @pl.when(h == GLA_HEADS - 1)
    def _finish():
        for r in range(tb // CUM_ROWS):
            rows = slice(r * CUM_ROWS, (r + 1) * CUM_ROWS)
            o_ref[rows, :] = _layer_norm_rows(o_ref[rows, :], lng_ref[...], lnb_ref[...])


def _gla_bwd(hmain, cum_b, o_f, x2, p, lng, lnb, batch, alpha):
    rows, d = x2.shape
    w_out = p["w_out"]
    key_dim = p["key_dim"]
    d_inner = w_out.shape[0]
    dk = key_dim // GLA_HEADS
    dv = d_inner // GLA_HEADS
    tb = min(TB_G, rows // batch)
    nt = rows // batch // tb
    grid = (batch, nt, GLA_HEADS)
    kq, kv = key_dim // dk, key_dim * 2 // dv
    kz = (2 * key_dim + d_inner) // dv

    def row(b, i):
        return b * nt + (nt - 1 - i)

    kernel = functools.partial(_gla_bwd_kernel, alpha=alpha, nt=nt)
    return pl.pallas_call(
        kernel,
        grid=grid,
        in_specs=[
            pl.BlockSpec((tb, dk), lambda b, i, h: (row(b, i), h)),
            pl.BlockSpec((tb, dk), lambda b, i, h: (row(b, i), kq + h)),
            pl.BlockSpec((tb, dv), lambda b, i, h: (row(b, i), kv + h)),
            pl.BlockSpec((tb, dk), lambda b, i, h: (row(b, i), h)),
            pl.BlockSpec((tb, dv), lambda b, i, h: (row(b, i), h)),
            pl.BlockSpec((tb, dv), lambda b, i, h: (row(b, i), kz + h)),
            pl.BlockSpec(memory_space=pl.ANY),
            pl.BlockSpec((dv, d), lambda b, i, h: (h, 0)),
            pl.BlockSpec((1, dv), lambda b, i, h: (0, h)),
            pl.BlockSpec((1, d), lambda b, i, h: (0, 0)),
            pl.BlockSpec((1, d), lambda b, i, h: (0, 0)),
        ],
        out_specs=pl.BlockSpec((tb, d), lambda b, i, h: (row(b, i), 0)),
        out_shape=jax.ShapeDtypeStruct((rows, d), F32),
        scratch_shapes=[
            pltpu.VMEM((GLA_HEADS, dk, dv), F32),
            pltpu.VMEM((tb, dv), BF16),
            pltpu.SemaphoreType.DMA,
        ],
        compiler_params=pltpu.CompilerParams(
            dimension_semantics=("arbitrary", "arbitrary", "arbitrary"),
            vmem_limit_bytes=VMEM_LIMIT),
        name="gla_bwd",
    )(hmain, hmain, hmain, cum_b, o_f, hmain, x2, w_out, p["gn"], lng, lnb)


def _tri_blocks(reverse):
    r = jnp.arange(CUM_ROWS)[:, None]
    c = jnp.arange(CUM_ROWS)[None, :]
    same = (r // MACRO_B) == (c // MACRO_B)
    tri = (c >= r) if reverse else (c <= r)
    return (same & tri).astype(BF16)


def _prep_a(w_in, vg, vb, w_s, b_s, w_out):
    return dict(w_in=_chunked_cols(w_in.astype(BF16), CW_A), w_s=w_s.astype(BF16),
                b_s=b_s[:, :, None].astype(F32), w_out=w_out.astype(BF16),
                vg=vg[None, :], vb=vb[None, :])


def _prep_b(w_in, w_g2, b_g, gn_g, w_out):
    key_dim = w_g2.shape[2]
    d_inner = w_out.shape[0]
    n_main = 2 * key_dim + 2 * d_inner
    zeros = jnp.zeros((GATE_RANK, key_dim), F32)
    w_gate = jnp.concatenate([jnp.concatenate([w_g2[0], zeros], axis=1),
                              jnp.concatenate([zeros, w_g2[1]], axis=1)], axis=0)
    return dict(w_main=_chunked_cols(w_in[:, :n_main].astype(BF16), CW_P),
                w_gl=w_in[:, n_main:].astype(BF16),
                w_gate=w_gate.astype(BF16), b_gate=b_g.reshape(1, 2 * key_dim),
                gn=gn_g[None, :], w_out=w_out.astype(BF16), key_dim=key_dim, d_inner=d_inner,
                tri_f=_tri_blocks(False), tri_b=_tri_blocks(True))


def _mixer_b_layer(x2, batch, p, lng, lnb, alpha):
    hmain, cum_f, cum_b = _proj_b(x2, p)
    o_f = _gla_fwd(hmain, cum_f, batch, p["key_dim"], p["d_inner"])
    return _gla_bwd(hmain, cum_b, o_f, x2, p, lng, lnb, batch, alpha)


def _trunk(x, layers, ln_g, ln_b, alpha):
    batch, seq, d = x.shape
    x2 = x.reshape(batch * seq, d)
    for i, (kind, p) in enumerate(layers):
        lng, lnb = ln_g[i][None, :], ln_b[i][None, :]
        if kind == "a":
            x2 = _layer_a(x2, p, lng, lnb, alpha)
        else:
            x2 = _mixer_b_layer(x2, batch, p, lng, lnb, alpha)
    return x2.reshape(batch, seq, d)


def kernel(x_prompt, x_sample, w_in_a, ln_v_g_a, ln_v_b_a, w_s_a, b_s_a, w_out_a, w_in_b, w_g2_b, b_g_b, gn_g_b, w_out_b, ln_g, ln_b):
    depth = ln_g.shape[0]
    alpha = (2 * depth) ** 0.25
    layers = []
    for i in range(depth):
        j = i // 2
        if i % 2 == 0:
            layers.append(("a", _prep_a(w_in_a[j], ln_v_g_a[j], ln_v_b_a[j], w_s_a[j], b_s_a[j], w_out_a[j])))
        else:
            layers.append(("b", _prep_b(w_in_b[j], w_g2_b[j], b_g_b[j], gn_g_b[j], w_out_b[j])))
    y_prompt = _trunk(x_prompt, layers, ln_g, ln_b, alpha)
    y_sample = _trunk(x_sample, layers, ln_g, ln_b, alpha)
    return (y_prompt, y_sample)
```

```python
import functools

import jax
import jax.numpy as jnp
from jax import lax
from jax.experimental import pallas as pl
from jax.experimental.pallas import tpu as pltpu

F32 = jnp.float32
BF16 = jnp.bfloat16

CHUNK_A = 128
N_GROUPS_A = 16
GLA_HEADS = 4
GATE_RANK = 16
GATE_TAU = 16.0
SUB_B = 64
LN_EPS = 1e-5
RMS_EPS = 1e-6

TM_A = 1024
CW_A = 512
RB_A = 256
TM_P = 1024
CW_P = 1024
TB_G = 1024
MACRO_B = 128
CUM_ROWS = 256
VMEM_LIMIT = 60 * 1024 * 1024


def _dot(a, b):
    return jnp.dot(a, b, preferred_element_type=F32)


def _dot_nt(a, b):
    return lax.dot_general(a, b, (((1,), (1,)), ((), ())), preferred_element_type=F32)


def _dot_tn(a, b):
    return lax.dot_general(a, b, (((0,), (0,)), ((), ())), preferred_element_type=F32)


def _gelu_tanh(x):
    c = 0.7978845608028654
    return 0.5 * x * (1.0 + jnp.tanh(c * (x + 0.044715 * (x * x * x))))


def _silu(x):
    return x / (1.0 + jnp.exp(-x))


def _log_sigmoid(x):
    return jnp.minimum(x, 0.0) - jnp.log(1.0 + jnp.exp(-jnp.abs(x)))


def _layer_norm_rows(h, g, b):
    mu = jnp.mean(h, axis=-1, keepdims=True)
    hc = h - mu
    var = jnp.mean(hc * hc, axis=-1, keepdims=True)
    return hc * lax.rsqrt(var + LN_EPS) * g + b


def _chunked_cols(w, cw):
    k, n = w.shape
    return w.reshape(k, n // cw, cw).transpose(1, 0, 2)


def _layer_a_kernel(xb_ref, x_hbm, w1_ref, wo_ref, ws_ref, bs_ref, vg_ref, vb_ref,
                    lng_ref, lnb_ref, o_ref,
                    gv_ref, sum_ref, sq_ref, mu_ref, rstd_ref, y_ref, sem, *, nc, d_inner, alpha):
    i = pl.program_id(0)
    j = pl.program_id(1)
    tm = o_ref.shape[0]
    cw = w1_ref.shape[2]
    gdim = d_inner // N_GROUPS_A
    npv = nc // 2
    lanes = sum_ref.shape[1]

    def x_copy():
        return pltpu.make_async_copy(x_hbm.at[pl.ds(i * tm, tm), :], o_ref, sem)

    @pl.when(j == 0)
    def _init():
        x_copy().start()
        sum_ref[...] = jnp.zeros_like(sum_ref)
        sq_ref[...] = jnp.zeros_like(sq_ref)

    @pl.when(j < npv)
    def _phase_v():
        for t in range(2):
            gv = _gelu_tanh(_dot(xb_ref[...], w1_ref[t]))
            gv_ref[2 * j + t] = gv.astype(BF16)
            part = gv[:, 0:lanes]
            part2 = part * part
            for l in range(1, cw // lanes):
                blk = gv[:, l * lanes:(l + 1) * lanes]
                part = part + blk
                part2 = part2 + blk * blk
            sum_ref[...] += part
            sq_ref[...] += part2

    @pl.when(j == npv)
    def _start_acc():
        mu = jnp.sum(sum_ref[...], axis=1, keepdims=True) * (1.0 / d_inner)
        var = jnp.sum(sq_ref[...], axis=1, keepdims=True) * (1.0 / d_inner) - mu * mu
        mu_ref[...] = mu
        rstd_ref[...] = lax.rsqrt(var + LN_EPS)
        x_copy().wait()
        o_ref[...] = alpha * o_ref[...]

    @pl.when(j >= npv)
    def _phase_mix():
        c = j - npv
        xb = xb_ref[...]
        gate = _gelu_tanh(_dot(xb, w1_ref[0])) * _silu(_dot(xb, w1_ref[1]))
        vn = ((gv_ref[c].astype(F32) - mu_ref[...]) * rstd_ref[...] * vg_ref[...]
              + vb_ref[...]).astype(BF16)
        for g in range(cw // gdim):
            cols = slice(g * gdim, (g + 1) * gdim)
            wsg = ws_ref[g]
            bsg = jnp.broadcast_to(bs_ref[g], (CHUNK_A, gdim))
            for n in range(tm // CHUNK_A):
                rows = slice(n * CHUNK_A, (n + 1) * CHUNK_A)
                s = _dot(wsg, vn[rows, cols]) + bsg
                y_ref[rows, cols] = (gate[rows, cols] * s).astype(BF16)
        o_ref[...] += _dot(y_ref[...], wo_ref[...])

    @pl.when(j == npv + nc - 1)
    def _finish():
        for r in range(tm // RB_A):
            rows = slice(r * RB_A, (r + 1) * RB_A)
            o_ref[rows, :] = _layer_norm_rows(o_ref[rows, :], lng_ref[...], lnb_ref[...])


def _layer_a(x2, p, lng, lnb, alpha):
    rows, d = x2.shape
    w1, w_out = p["w1"], p["w_out"]
    d_inner = w_out.shape[0]
    cw = w1.shape[3]
    nc = d_inner // cw
    npv = nc // 2
    gpc = cw // (d_inner // N_GROUPS_A)
    tm = min(TM_A, rows)
    grid = (rows // tm, npv + nc)
    xb = x2.astype(BF16)

    def cmix(j):
        return jnp.maximum(j - npv, 0)

    kernel = functools.partial(_layer_a_kernel, nc=nc, d_inner=d_inner, alpha=alpha)
    return pl.pallas_call(
        kernel,
        grid=grid,
        in_specs=[
            pl.BlockSpec((tm, d), lambda i, j: (i, 0)),
            pl.BlockSpec(memory_space=pl.ANY),
            pl.BlockSpec((None, 2, d, cw), lambda i, j: (j, 0, 0, 0)),
            pl.BlockSpec((cw, d), lambda i, j: (cmix(j), 0)),
            pl.BlockSpec((gpc, CHUNK_A, CHUNK_A), lambda i, j: (cmix(j), 0, 0)),
            pl.BlockSpec((gpc, CHUNK_A, 1), lambda i, j: (cmix(j), 0, 0)),
            pl.BlockSpec((1, cw), lambda i, j: (0, cmix(j))),
            pl.BlockSpec((1, cw), lambda i, j: (0, cmix(j))),
            pl.BlockSpec((1, d), lambda i, j: (0, 0)),
            pl.BlockSpec((1, d), lambda i, j: (0, 0)),
        ],
        out_specs=pl.BlockSpec((tm, d), lambda i, j: (i, 0)),
        out_shape=jax.ShapeDtypeStruct((rows, d), F32),
        scratch_shapes=[
            pltpu.VMEM((nc, tm, cw), BF16),
            pltpu.VMEM((tm, 128), F32),
            pltpu.VMEM((tm, 128), F32),
            pltpu.VMEM((tm, 1), F32),
            pltpu.VMEM((tm, 1), F32),
            pltpu.VMEM((tm, cw), BF16),
            pltpu.SemaphoreType.DMA,
        ],
        compiler_params=pltpu.CompilerParams(
            dimension_semantics=("arbitrary", "arbitrary"),
            vmem_limit_bytes=VMEM_LIMIT),
        name="layer_a",
    )(xb, x2, w1, w_out, p["w_s"], p["b_s"], p["vg"], p["vb"], lng, lnb)


def _proj_b_kernel(x_ref, w_ref, wgl_ref, wg_ref, bg_ref, lf_ref, lb_ref,
                   h_ref, gf_ref, rb_ref, xb_ref, *, key_dim):
    j = pl.program_id(1)
    tm = x_ref.shape[0]

    @pl.when(j == 0)
    def _gates():
        for r in range(tm // CUM_ROWS):
            rows = slice(r * CUM_ROWS, (r + 1) * CUM_ROWS)
            xb = x_ref[rows, :].astype(BF16)
            xb_ref[rows, :] = xb
            gl = _dot(xb, wgl_ref[...])
            pre = _dot(gl.astype(BF16), wg_ref[...]) + bg_ref[...]
            glog = _log_sigmoid(pre) * (1.0 / GATE_TAU)
            for dst, tri, cols in ((gf_ref, lf_ref, slice(0, key_dim)),
                                   (rb_ref, lb_ref, slice(key_dim, 2 * key_dim))):
                gb = glog[:, cols]
                hi = gb.astype(BF16)
                lo = (gb - hi.astype(F32)).astype(BF16)
                dst[rows, :] = _dot(tri[...], hi) + _dot(tri[...], lo)

    h_ref[...] = _dot(xb_ref[...], w_ref[...]).astype(BF16)


def _proj_b(x2, p):
    rows, d = x2.shape
    w_main = p["w_main"]
    n_chunks, _, cw = w_main.shape
    key_dim = p["key_dim"]
    tm = min(TM_P, rows)
    grid = (rows // tm, n_chunks)
    kernel = functools.partial(_proj_b_kernel, key_dim=key_dim)
    small = [p["w_gl"], p["w_gate"], p["b_gate"], p["tri_f"], p["tri_b"]]
    return pl.pallas_call(
        kernel,
        grid=grid,
        in_specs=[
            pl.BlockSpec((tm, d), lambda i, j: (i, 0)),
            pl.BlockSpec((None, d, cw), lambda i, j: (j, 0, 0)),
        ] + [pl.BlockSpec(a.shape, lambda i, j: (0, 0)) for a in small],
        out_specs=[
            pl.BlockSpec((tm, cw), lambda i, j: (i, j)),
            pl.BlockSpec((tm, key_dim), lambda i, j: (i, 0)),
            pl.BlockSpec((tm, key_dim), lambda i, j: (i, 0)),
        ],
        out_shape=[
            jax.ShapeDtypeStruct((rows, n_chunks * cw), BF16),
            jax.ShapeDtypeStruct((rows, key_dim), F32),
            jax.ShapeDtypeStruct((rows, key_dim), F32),
        ],
        scratch_shapes=[pltpu.VMEM((tm, d), BF16)],
        compiler_params=pltpu.CompilerParams(
            dimension_semantics=("arbitrary", "arbitrary"),
            vmem_limit_bytes=VMEM_LIMIT),
        name="proj_b",
    )(x2, w_main, *small)


def _row_to_col(vec_row, width):
    n = vec_row.shape[1]
    return jnp.broadcast_to(vec_row, (width, n)).T


def _gla_macro_chunk(q, k, v, cum, state, reverse):
    c_rows, dk = q.shape
    ns = c_rows // SUB_B
    qs = q.astype(F32) * (dk ** -0.5)
    kf = k.astype(F32)
    order = list(range(ns))[::-1] if reverse else list(range(ns))
    mid_row = SUB_B // 2 - 1 if reverse else SUB_B // 2
    end_row = 0 if reverse else SUB_B - 1

    rsl = {a: slice(a * SUB_B, (a + 1) * SUB_B) for a in range(ns)}
    r_end, q_diag, k_diag, q_off, k_off = {}, {}, {}, {}, {}
    r_prev = {}
    zero_row = jnp.zeros((1, dk), F32)
    prev = zero_row
    for a in order:
        ca = cum[rsl[a]]
        mid = ca[mid_row:mid_row + 1]
        r_end[a] = ca[end_row:end_row + 1]
        r_prev[a] = prev
        q_diag[a] = (qs[rsl[a]] * jnp.exp(ca - mid)).astype(BF16)
        k_diag[a] = (kf[rsl[a]] * jnp.exp(mid - ca)).astype(BF16)
        q_off[a] = qs[rsl[a]] * jnp.exp(ca - prev)
        k_off[a] = kf[rsl[a]] * jnp.exp(r_end[a] - ca)
        prev = r_end[a]
    r_last = prev

    ri = lax.broadcasted_iota(jnp.int32, (SUB_B, SUB_B), 0)
    ci = lax.broadcasted_iota(jnp.int32, (SUB_B, SUB_B), 1)
    keep = (ci > ri) if reverse else (ci <= ri)

    k_off_b = {a: k_off[a].astype(BF16) for a in range(ns)}
    a_rows = []
    q_state = []
    for a in range(ns):
        blocks = []
        for b in range(ns):
            if b == a:
                blk = jnp.where(keep, _dot_nt(q_diag[a], k_diag[a]), 0.0)
            elif order.index(b) < order.index(a):
                if order.index(b) + 1 == order.index(a):
                    qa = q_off[a]
                else:
                    qa = q_off[a] * jnp.exp(r_prev[a] - r_end[b])
                blk = _dot_nt(qa.astype(BF16), k_off_b[b])
            else:
                blk = jnp.zeros((SUB_B, SUB_B), F32)
            blocks.append(blk)
        a_rows.append(jnp.concatenate(blocks, axis=1) if ns > 1 else blocks[0])
        q_state.append(q_off[a] * jnp.exp(r_prev[a]))
    a_full = jnp.concatenate(a_rows, axis=0) if ns > 1 else a_rows[0]
    q_st = jnp.concatenate(q_state, axis=0) if ns > 1 else q_state[0]

    o = _dot(a_full.astype(BF16), v) + _dot(q_st.astype(BF16), state.astype(BF16))

    k_dec = [k_off[a] * jnp.exp(r_last - r_end[a]) for a in range(ns)]
    k_dec = (jnp.concatenate(k_dec, axis=0) if ns > 1 else k_dec[0]).astype(BF16)
    upd = _dot_tn(k_dec, v)
    dcol = _row_to_col(jnp.exp(r_last), 128)
    dv = v.shape[1]
    new_state = jnp.concatenate(
        [state[:, n * 128:(n + 1) * 128] * dcol for n in range(dv // 128)], axis=1) + upd
    return o, new_state


def _gla_sweep_tile(q_ref, k_ref, v_ref, cum_ref, s_ref, h, emit, reverse):
    tb = q_ref.shape[0]
    nm = tb // MACRO_B
    chunks = range(nm - 1, -1, -1) if reverse else range(nm)
    for m in chunks:
        rows = slice(m * MACRO_B, (m + 1) * MACRO_B)
        o, s_new = _gla_macro_chunk(q_ref[rows, :], k_ref[rows, :], v_ref[rows, :], cum_ref[rows, :],
                                    s_ref[h], reverse)
        s_ref[h] = s_new
        emit(rows, o)


def _gla_fwd_kernel(q_ref, k_ref, v_ref, cum_ref, o_ref, s_ref):
    i = pl.program_id(1)
    h = pl.program_id(2)

    @pl.when(i == 0)
    def _reset():
        s_ref[h] = jnp.zeros(s_ref.shape[1:], F32)

    def emit(rows, o):
        o_ref[rows, :] = o.astype(o_ref.dtype)

    _gla_sweep_tile(q_ref, k_ref, v_ref, cum_ref, s_ref, h, emit, reverse=False)


def _gla_fwd(hmain, cum_f, batch, key_dim, d_inner):
    rows = hmain.shape[0]
    dk = key_dim // GLA_HEADS
    dv = d_inner // GLA_HEADS
    tb = min(TB_G, rows // batch)
    nt = rows // batch // tb
    grid = (batch, nt, GLA_HEADS)
    kq, kv = key_dim // dk, key_dim * 2 // dv
    return pl.pallas_call(
        _gla_fwd_kernel,
        grid=grid,
        in_specs=[
            pl.BlockSpec((tb, dk), lambda b, i, h: (b * nt + i, h)),
            pl.BlockSpec((tb, dk), lambda b, i, h: (b * nt + i, kq + h)),
            pl.BlockSpec((tb, dv), lambda b, i, h: (b * nt + i, kv + h)),
            pl.BlockSpec((tb, dk), lambda b, i, h: (b * nt + i, h)),
        ],
        out_specs=pl.BlockSpec((tb, dv), lambda b, i, h: (b * nt + i, h)),
        out_shape=jax.ShapeDtypeStruct((rows, d_inner), BF16),
        scratch_shapes=[pltpu.VMEM((GLA_HEADS, dk, dv), F32)],
        compiler_params=pltpu.CompilerParams(
            dimension_semantics=("arbitrary", "arbitrary", "arbitrary"),
            vmem_limit_bytes=VMEM_LIMIT),
        name="gla_fwd",
    )(hmain, hmain, hmain, cum_f)


def _gla_bwd_kernel(q_ref, k_ref, v_ref, cum_ref, of_ref, z_ref, x_hbm, wo_ref, gn_ref,
                    lng_ref, lnb_ref, o_ref, s_ref, y_ref, sem, *, alpha, nt):
    b = pl.program_id(0)
    i = pl.program_id(1)
    h = pl.program_id(2)
    tb = o_ref.shape[0]

    def x_copy():
        row0 = (b * nt + (nt - 1 - i)) * tb
        return pltpu.make_async_copy(x_hbm.at[pl.ds(row0, tb), :], o_ref, sem)

    @pl.when(i == 0)
    def _reset():
        s_ref[h] = jnp.zeros(s_ref.shape[1:], F32)

    @pl.when(h == 0)
    def _fetch_x():
        x_copy().start()

    def emit(rows, o_b):
        o = of_ref[rows, :].astype(F32) + o_b
        o = o * lax.rsqrt(jnp.mean(o * o, axis=-1, keepdims=True) + RMS_EPS) * gn_ref[...]
        y_ref[rows, :] = (o * _silu(z_ref[rows, :].astype(F32))).astype(BF16)

    _gla_sweep_tile(q_ref, k_ref, v_ref, cum_ref, s_ref, h, emit, reverse=True)

    @pl.when(h == 0)
    def _start_acc():
        x_copy().wait()
        o_ref[...] = alpha * o_ref[...]

    o_ref[...] += _dot(y_ref[...], wo_ref[...])

    @pl.when(h == GLA_HEADS - 1)
    def _finish():
        for r in range(tb // CUM_ROWS):
            rows = slice(r * CUM_ROWS, (r + 1) * CUM_ROWS)
            o_ref[rows, :] = _layer_norm_rows(o_ref[rows, :], lng_ref[...], lnb_ref[...])


def _gla_bwd(hmain, cum_b, o_f, x2, p, lng, lnb, batch, alpha):
    rows, d = x2.shape
    w_out = p["w_out"]
    key_dim = p["key_dim"]
    d_inner = w_out.shape[0]
    dk = key_dim // GLA_HEADS
    dv = d_inner // GLA_HEADS
    tb = min(TB_G, rows // batch)
    nt = rows // batch // tb
    grid = (batch, nt, GLA_HEADS)
    kq, kv = key_dim // dk, key_dim * 2 // dv
    kz = (2 * key_dim + d_inner) // dv

    def row(b, i):
        return b * nt + (nt - 1 - i)

    kernel = functools.partial(_gla_bwd_kernel, alpha=alpha, nt=nt)
    return pl.pallas_call(
        kernel,
        grid=grid,
        in_specs=[
            pl.BlockSpec((tb, dk), lambda b, i, h: (row(b, i), h)),
            pl.BlockSpec((tb, dk), lambda b, i, h: (row(b, i), kq + h)),
            pl.BlockSpec((tb, dv), lambda b, i, h: (row(b, i), kv + h)),
            pl.BlockSpec((tb, dk), lambda b, i, h: (row(b, i), h)),
            pl.BlockSpec((tb, dv), lambda b, i, h: (row(b, i), h)),
            pl.BlockSpec((tb, dv), lambda b, i, h: (row(b, i), kz + h)),
            pl.BlockSpec(memory_space=pl.ANY),
            pl.BlockSpec((dv, d), lambda b, i, h: (h, 0)),
            pl.BlockSpec((1, dv), lambda b, i, h: (0, h)),
            pl.BlockSpec((1, d), lambda b, i, h: (0, 0)),
            pl.BlockSpec((1, d), lambda b, i, h: (0, 0)),
        ],
        out_specs=pl.BlockSpec((tb, d), lambda b, i, h: (row(b, i), 0)),
        out_shape=jax.ShapeDtypeStruct((rows, d), F32),
        scratch_shapes=[
            pltpu.VMEM((GLA_HEADS, dk, dv), F32),
            pltpu.VMEM((tb, dv), BF16),
            pltpu.SemaphoreType.DMA,
        ],
        compiler_params=pltpu.CompilerParams(
            dimension_semantics=("arbitrary", "arbitrary", "arbitrary"),
            vmem_limit_bytes=VMEM_LIMIT),
        name="gla_bwd",
    )(hmain, hmain, hmain, cum_b, o_f, hmain, x2, w_out, p["gn"], lng, lnb)


def _tri_blocks(reverse):
    r = jnp.arange(CUM_ROWS)[:, None]
    c = jnp.arange(CUM_ROWS)[None, :]
    same = (r // MACRO_B) == (c // MACRO_B)
    tri = (c >= r) if reverse else (c <= r)
    return (same & tri).astype(BF16)


def _prep_a(w_in, vg, vb, w_s, b_s, w_out):
    d_inner = w_out.shape[0]
    nc = d_inner // CW_A
    wc = _chunked_cols(w_in.astype(BF16), CW_A)
    d = wc.shape[1]
    v_pairs = wc[nc:2 * nc].reshape(nc // 2, 2, d, CW_A)
    uz = jnp.stack([wc[:nc], wc[2 * nc:]], axis=1)
    return dict(w1=jnp.concatenate([v_pairs, uz], axis=0), w_s=w_s.astype(BF16),
                b_s=b_s[:, :, None].astype(F32), w_out=w_out.astype(BF16),
                vg=vg[None, :], vb=vb[None, :])


def _prep_b(w_in, w_g2, b_g, gn_g, w_out):
    key_dim = w_g2.shape[2]
    d_inner = w_out.shape[0]
    n_main = 2 * key_dim + 2 * d_inner
    zeros = jnp.zeros((GATE_RANK, key_dim), F32)
    w_gate = jnp.concatenate([jnp.concatenate([w_g2[0], zeros], axis=1),
                              jnp.concatenate([zeros, w_g2[1]], axis=1)], axis=0)
    return dict(w_main=_chunked_cols(w_in[:, :n_main].astype(BF16), CW_P),
                w_gl=w_in[:, n_main:].astype(BF16),
                w_gate=w_gate.astype(BF16), b_gate=b_g.reshape(1, 2 * key_dim),
                gn=gn_g[None, :], w_out=w_out.astype(BF16), key_dim=key_dim, d_inner=d_inner,
                tri_f=_tri_blocks(False), tri_b=_tri_blocks(True))


def _mixer_b_layer(x2, batch, p, lng, lnb, alpha):
    hmain, cum_f, cum_b = _proj_b(x2, p)
    o_f = _gla_fwd(hmain, cum_f, batch, p["key_dim"], p["d_inner"])
    return _gla_bwd(hmain, cum_b, o_f, x2, p, lng, lnb, batch, alpha)


def _trunk(x, layers, ln_g, ln_b, alpha):
    batch, seq, d = x.shape
    x2 = x.reshape(batch * seq, d)
    for i, (kind, p) in enumerate(layers):
        lng, lnb = ln_g[i][None, :], ln_b[i][None, :]
        if kind == "a":
            x2 = _layer_a(x2, p, lng, lnb, alpha)
        else:
            x2 = _mixer_b_layer(x2, batch, p, lng, lnb, alpha)
    return x2.reshape(batch, seq, d)


def kernel(x_prompt, x_sample, w_in_a, ln_v_g_a, ln_v_b_a, w_s_a, b_s_a, w_out_a, w_in_b, w_g2_b, b_g_b, gn_g_b, w_out_b, ln_g, ln_b):
    depth = ln_g.shape[0]
    alpha = (2 * depth) ** 0.25
    layers = []
    for i in range(depth):
        j = i // 2
        if i % 2 == 0:
            layers.append(("a", _prep_a(w_in_a[j], ln_v_g_a[j], ln_v_b_a[j], w_s_a[j], b_s_a[j], w_out_a[j])))
        else:
            layers.append(("b", _prep_b(w_in_b[j], w_g2_b[j], b_g_b[j], gn_g_b[j], w_out_b[j])))
    y_prompt = _trunk(x_prompt, layers, ln_g, ln_b, alpha)
    y_sample = _trunk(x_sample, layers, ln_g, ln_b, alpha)
    return (y_prompt, y_sample)
```

```python
import functools

import jax
import jax.numpy as jnp
from jax import lax
from jax.experimental import pallas as pl
from jax.experimental.pallas import tpu as pltpu

F32 = jnp.float32
BF16 = jnp.bfloat16

CHUNK_A = 128
N_GROUPS_A = 16
GLA_HEADS = 4
GATE_RANK = 16
GATE_TAU = 16.0
SUB_B = 64
LN_EPS = 1e-5
RMS_EPS = 1e-6

TM_A = 1024
CW_A = 512
RB_A = 256
TM_P = 1024
CW_P = 1024
TB_G = 1024
MACRO_B = 256
CUM_ROWS = 256
COL_B = 256
VMEM_LIMIT = 60 * 1024 * 1024


def _dot(a, b):
    return jnp.dot(a, b, preferred_element_type=F32)


def _dot_nt(a, b):
    return lax.dot_general(a, b, (((1,), (1,)), ((), ())), preferred_element_type=F32)


def _dot_tn(a, b):
    return lax.dot_general(a, b, (((0,), (0,)), ((), ())), preferred_element_type=F32)


def _gelu_tanh(x):
    c = 0.7978845608028654
    return 0.5 * x * (1.0 + jnp.tanh(c * (x + 0.044715 * (x * x * x))))


def _silu(x):
    return 0.5 * x * (1.0 + jnp.tanh(0.5 * x))


def _log_sigmoid(x):
    return jnp.minimum(x, 0.0) - jnp.log(1.0 + jnp.exp(-jnp.abs(x)))


def _layer_norm_rows(h, g, b):
    mu = jnp.mean(h, axis=-1, keepdims=True)
    hc = h - mu
    var = jnp.mean(hc * hc, axis=-1, keepdims=True)
    return hc * lax.rsqrt(var + LN_EPS) * g + b


def _chunked_cols(w, cw):
    k, n = w.shape
    return w.reshape(k, n // cw, cw).transpose(1, 0, 2)


def _layer_a_kernel(xb_ref, x_hbm, w1_ref, wo_ref, ws_ref, bs_ref, vg_ref, vb_ref,
                    lng_ref, lnb_ref, o_ref,
                    gv_ref, sum_ref, sq_ref, mu_ref, rstd_ref, y_ref, sem, *, nc, d_inner, alpha):
    i = pl.program_id(0)
    j = pl.program_id(1)
    tm = o_ref.shape[0]
    cw = w1_ref.shape[2]
    gdim = d_inner // N_GROUPS_A
    npv = nc // 2
    lanes = sum_ref.shape[1]

    def x_copy():
        return pltpu.make_async_copy(x_hbm.at[pl.ds(i * tm, tm), :], o_ref, sem)

    @pl.when(j == 0)
    def _init():
        x_copy().start()
        sum_ref[...] = jnp.zeros_like(sum_ref)
        sq_ref[...] = jnp.zeros_like(sq_ref)

    @pl.when(j < npv)
    def _phase_v():
        for t in range(2):
            gv = _gelu_tanh(_dot(xb_ref[...], w1_ref[t]))
            gv_ref[2 * j + t] = gv.astype(BF16)
            part = gv[:, 0:lanes]
            part2 = part * part
            for l in range(1, cw // lanes):
                blk = gv[:, l * lanes:(l + 1) * lanes]
                part = part + blk
                part2 = part2 + blk * blk
            sum_ref[...] += part
            sq_ref[...] += part2

    @pl.when(j == npv)
    def _start_acc():
        mu = jnp.sum(sum_ref[...], axis=1, keepdims=True) * (1.0 / d_inner)
        var = jnp.sum(sq_ref[...], axis=1, keepdims=True) * (1.0 / d_inner) - mu * mu
        mu_ref[...] = mu
        rstd_ref[...] = lax.rsqrt(var + LN_EPS)
        x_copy().wait()
        o_ref[...] = alpha * o_ref[...]

    @pl.when(j >= npv)
    def _phase_mix():
        c = j - npv
        xb = xb_ref[...]
        gate = _gelu_tanh(_dot(xb, w1_ref[0])) * _silu(_dot(xb, w1_ref[1]))
        vn = ((gv_ref[c].astype(F32) - mu_ref[...]) * rstd_ref[...] * vg_ref[...]
              + vb_ref[...]).astype(BF16)
        for g in range(cw // gdim):
            cols = slice(g * gdim, (g + 1) * gdim)
            wsg = ws_ref[g]
            bsg = jnp.broadcast_to(bs_ref[g], (CHUNK_A, gdim))
            for n in range(tm // CHUNK_A):
                rows = slice(n * CHUNK_A, (n + 1) * CHUNK_A)
                s = _dot(wsg, vn[rows, cols]) + bsg
                y_ref[rows, cols] = (gate[rows, cols] * s).astype(BF16)
        o_ref[...] += _dot(y_ref[...], wo_ref[...])

    @pl.when(j == npv + nc - 1)
    def _finish():
        for r in range(tm // RB_A):
            rows = slice(r * RB_A, (r + 1) * RB_A)
            o_ref[rows, :] = _layer_norm_rows(o_ref[rows, :], lng_ref[...], lnb_ref[...])


def _layer_a(x2, p, lng, lnb, alpha):
    rows, d = x2.shape
    w1, w_out = p["w1"], p["w_out"]
    d_inner = w_out.shape[0]
    cw = w1.shape[3]
    nc = d_inner // cw
    npv = nc // 2
    gpc = cw // (d_inner // N_GROUPS_A)
    tm = min(TM_A, rows)
    grid = (rows // tm, npv + nc)
    xb = x2.astype(BF16)

    def cmix(j):
        return jnp.maximum(j - npv, 0)

    kernel = functools.partial(_layer_a_kernel, nc=nc, d_inner=d_inner, alpha=alpha)
    return pl.pallas_call(
        kernel,
        grid=grid,
        in_specs=[
            pl.BlockSpec((tm, d), lambda i, j: (i, 0)),
            pl.BlockSpec(memory_space=pl.ANY),
            pl.BlockSpec((None, 2, d, cw), lambda i, j: (j, 0, 0, 0)),
            pl.BlockSpec((cw, d), lambda i, j: (cmix(j), 0)),
            pl.BlockSpec((gpc, CHUNK_A, CHUNK_A), lambda i, j: (cmix(j), 0, 0)),
            pl.BlockSpec((gpc, CHUNK_A, 1), lambda i, j: (cmix(j), 0, 0)),
            pl.BlockSpec((1, cw), lambda i, j: (0, cmix(j))),
            pl.BlockSpec((1, cw), lambda i, j: (0, cmix(j))),
            pl.BlockSpec((1, d), lambda i, j: (0, 0)),
            pl.BlockSpec((1, d), lambda i, j: (0, 0)),
        ],
        out_specs=pl.BlockSpec((tm, d), lambda i, j: (i, 0)),
        out_shape=jax.ShapeDtypeStruct((rows, d), F32),
        scratch_shapes=[
            pltpu.VMEM((nc, tm, cw), BF16),
            pltpu.VMEM((tm, 128), F32),
            pltpu.VMEM((tm, 128), F32),
            pltpu.VMEM((tm, 1), F32),
            pltpu.VMEM((tm, 1), F32),
            pltpu.VMEM((tm, cw), BF16),
            pltpu.SemaphoreType.DMA,
        ],
        compiler_params=pltpu.CompilerParams(
            dimension_semantics=("arbitrary", "arbitrary"),
            vmem_limit_bytes=VMEM_LIMIT),
        name="layer_a",
    )(xb, x2, w1, w_out, p["w_s"], p["b_s"], p["vg"], p["vb"], lng, lnb)


def _proj_b_kernel(x_ref, w_ref, wgl_ref, wg_ref, bg_ref, lf_ref, lb_ref,
                   h_ref, gf_ref, rb_ref, xb_ref, *, key_dim):
    j = pl.program_id(1)
    tm = x_ref.shape[0]

    @pl.when(j == 0)
    def _gates():
        for r in range(tm // CUM_ROWS):
            rows = slice(r * CUM_ROWS, (r + 1) * CUM_ROWS)
            xb = x_ref[rows, :].astype(BF16)
            xb_ref[rows, :] = xb
            gl = _dot(xb, wgl_ref[...])
            pre = _dot(gl.astype(BF16), wg_ref[...]) + bg_ref[...]
            glog = _log_sigmoid(pre) * (1.0 / GATE_TAU)
            for dst, tri, cols in ((gf_ref, lf_ref, slice(0, key_dim)),
                                   (rb_ref, lb_ref, slice(key_dim, 2 * key_dim))):
                gb = glog[:, cols]
                hi = gb.astype(BF16)
                lo = (gb - hi.astype(F32)).astype(BF16)
                cs = _dot(tri[...], hi) + _dot(tri[...], lo)
                for c in range(key_dim // COL_B):
                    dst[c, rows, :] = cs[:, c * COL_B:(c + 1) * COL_B]

    res = _dot(xb_ref[...], w_ref[...])
    for c in range(res.shape[1] // COL_B):
        h_ref[c] = res[:, c * COL_B:(c + 1) * COL_B].astype(BF16)


def _proj_b(x2, p):
    rows, d = x2.shape
    w_main = p["w_main"]
    n_chunks, _, cw = w_main.shape
    key_dim = p["key_dim"]
    tm = min(TM_P, rows)
    grid = (rows // tm, n_chunks)
    kernel = functools.partial(_proj_b_kernel, key_dim=key_dim)
    small = [p["w_gl"], p["w_gate"], p["b_gate"], p["tri_f"], p["tri_b"]]
    return pl.pallas_call(
        kernel,
        grid=grid,
        in_specs=[
            pl.BlockSpec((tm, d), lambda i, j: (i, 0)),
            pl.BlockSpec((None, d, cw), lambda i, j: (j, 0, 0)),
        ] + [pl.BlockSpec(a.shape, lambda i, j: (0, 0)) for a in small],
        out_specs=[
            pl.BlockSpec((cw // COL_B, tm, COL_B), lambda i, j: (j, i, 0)),
            pl.BlockSpec((key_dim // COL_B, tm, COL_B), lambda i, j: (0, i, 0)),
            pl.BlockSpec((key_dim // COL_B, tm, COL_B), lambda i, j: (0, i, 0)),
        ],
        out_shape=[
            jax.ShapeDtypeStruct((n_chunks * cw // COL_B, rows, COL_B), BF16),
            jax.ShapeDtypeStruct((key_dim // COL_B, rows, COL_B), F32),
            jax.ShapeDtypeStruct((key_dim // COL_B, rows, COL_B), F32),
        ],
        scratch_shapes=[pltpu.VMEM((tm, d), BF16)],
        compiler_params=pltpu.CompilerParams(
            dimension_semantics=("arbitrary", "arbitrary"),
            vmem_limit_bytes=VMEM_LIMIT),
        name="proj_b",
    )(x2, w_main, *small)


def _row_to_col(vec_row, width):
    n = vec_row.shape[1]
    return jnp.broadcast_to(vec_row, (width, n)).T


def _gla_macro_chunk(q, k, v_blocks, cum, state_blocks, reverse):
    c_rows, dk = q.shape
    ns = c_rows // SUB_B
    qs = q.astype(F32) * (dk ** -0.5)
    kf = k.astype(F32)
    order = list(range(ns))[::-1] if reverse else list(range(ns))
    mid_row = SUB_B // 2 - 1 if reverse else SUB_B // 2
    end_row = 0 if reverse else SUB_B - 1

    rsl = {a: slice(a * SUB_B, (a + 1) * SUB_B) for a in range(ns)}
    r_end, q_diag, k_diag, q_off, k_off = {}, {}, {}, {}, {}
    r_prev = {}
    zero_row = jnp.zeros((1, dk), F32)
    prev = zero_row
    for a in order:
        ca = cum[rsl[a]]
        mid = ca[mid_row:mid_row + 1]
        r_end[a] = ca[end_row:end_row + 1]
        r_prev[a] = prev
        q_diag[a] = (qs[rsl[a]] * jnp.exp(ca - mid)).astype(BF16)
        k_diag[a] = (kf[rsl[a]] * jnp.exp(mid - ca)).astype(BF16)
        q_off[a] = qs[rsl[a]] * jnp.exp(ca - prev)
        k_off[a] = kf[rsl[a]] * jnp.exp(r_end[a] - ca)
        prev = r_end[a]
    r_last = prev

    ri = lax.broadcasted_iota(jnp.int32, (SUB_B, SUB_B), 0)
    ci = lax.broadcasted_iota(jnp.int32, (SUB_B, SUB_B), 1)
    keep = (ci > ri) if reverse else (ci <= ri)

    k_off_b = {a: k_off[a].astype(BF16) for a in range(ns)}
    a_rows = []
    q_state = []
    for a in range(ns):
        blocks = []
        for b in range(ns):
            if b == a:
                blk = jnp.where(keep, _dot_nt(q_diag[a], k_diag[a]), 0.0)
            elif order.index(b) < order.index(a):
                if order.index(b) + 1 == order.index(a):
                    qa = q_off[a]
                else:
                    qa = q_off[a] * jnp.exp(r_prev[a] - r_end[b])
                blk = _dot_nt(qa.astype(BF16), k_off_b[b])
            else:
                blk = jnp.zeros((SUB_B, SUB_B), F32)
            blocks.append(blk)
        a_rows.append(jnp.concatenate(blocks, axis=1) if ns > 1 else blocks[0])
        q_state.append(q_off[a] * jnp.exp(r_prev[a]))
    a_full = jnp.concatenate(a_rows, axis=0) if ns > 1 else a_rows[0]
    q_st = jnp.concatenate(q_state, axis=0) if ns > 1 else q_state[0]

    a_b = a_full.astype(BF16)
    q_b = q_st.astype(BF16)
    k_dec = [k_off[a] * jnp.exp(r_last - r_end[a]) for a in range(ns)]
    k_dec = (jnp.concatenate(k_dec, axis=0) if ns > 1 else k_dec[0]).astype(BF16)
    dcol = _row_to_col(jnp.exp(r_last), 128)
    dcol = jnp.concatenate([dcol] * (v_blocks[0].shape[1] // 128), axis=1)
    o_blocks, new_state = [], []
    for vc, sc in zip(v_blocks, state_blocks):
        o_blocks.append(_dot(a_b, vc) + _dot(q_b, sc.astype(BF16)))
        new_state.append(sc * dcol + _dot_tn(k_dec, vc))
    return o_blocks, new_state


def _gla_sweep_tile(q_ref, k_ref, v_ref, cum_ref, s_ref, h, emit, reverse):
    tb = q_ref.shape[0]
    nvb = v_ref.shape[0]
    nm = tb // MACRO_B
    chunks = range(nm - 1, -1, -1) if reverse else range(nm)
    for m in chunks:
        rows = slice(m * MACRO_B, (m + 1) * MACRO_B)
        o_blocks, s_new = _gla_macro_chunk(
            q_ref[rows, :], k_ref[rows, :], [v_ref[c, rows, :] for c in range(nvb)], cum_ref[rows, :],
            [s_ref[h, c] for c in range(nvb)], reverse)
        for c in range(nvb):
            s_ref[h, c] = s_new[c]
        emit(rows, o_blocks)


def _gla_fwd_kernel(q_ref, k_ref, v_ref, cum_ref, o_ref, s_ref):
    i = pl.program_id(1)
    h = pl.program_id(2)

    @pl.when(i == 0)
    def _reset():
        s_ref[h] = jnp.zeros(s_ref.shape[1:], F32)

    def emit(rows, o_blocks):
        for c, o in enumerate(o_blocks):
            o_ref[c, rows, :] = o.astype(o_ref.dtype)

    _gla_sweep_tile(q_ref, k_ref, v_ref, cum_ref, s_ref, h, emit, reverse=False)


def _gla_dims(hmain, batch, key_dim, d_inner):
    rows = hmain.shape[1]
    dk = key_dim // GLA_HEADS
    dv = d_inner // GLA_HEADS
    assert dk == COL_B and dv % COL_B == 0
    tb = min(TB_G, rows // batch)
    nt = rows // batch // tb
    nvb = dv // COL_B
    kk = key_dim // COL_B
    kv = 2 * key_dim // (nvb * COL_B)
    kz = (2 * key_dim + d_inner) // (nvb * COL_B)
    return rows, dk, dv, tb, nt, nvb, kk, kv, kz


def _gla_fwd(hmain, cum_f, batch, key_dim, d_inner):
    rows, dk, dv, tb, nt, nvb, kk, kv, _ = _gla_dims(hmain, batch, key_dim, d_inner)
    grid = (batch, nt, GLA_HEADS)
    return pl.pallas_call(
        _gla_fwd_kernel,
        grid=grid,
        in_specs=[
            pl.BlockSpec((None, tb, dk), lambda b, i, h: (h, b * nt + i, 0)),
            pl.BlockSpec((None, tb, dk), lambda b, i, h: (kk + h, b * nt + i, 0)),
            pl.BlockSpec((nvb, tb, COL_B), lambda b, i, h: (kv + h, b * nt + i, 0)),
            pl.BlockSpec((None, tb, dk), lambda b, i, h: (h, b * nt + i, 0)),
        ],
        out_specs=pl.BlockSpec((nvb, tb, COL_B), lambda b, i, h: (h, b * nt + i, 0)),
        out_shape=jax.ShapeDtypeStruct((GLA_HEADS * nvb, rows, COL_B), BF16),
        scratch_shapes=[pltpu.VMEM((GLA_HEADS, nvb, dk, COL_B), F32)],
        compiler_params=pltpu.CompilerParams(
            dimension_semantics=("arbitrary", "arbitrary", "arbitrary"),
            vmem_limit_bytes=VMEM_LIMIT),
        name="gla_fwd",
    )(hmain, hmain, hmain, cum_f)


def _gla_bwd_kernel(q_ref, k_ref, v_ref, cum_ref, of_ref, z_ref, x_hbm, wo_ref, gn_ref,
                    lng_ref, lnb_ref, o_ref, s_ref, y_ref, sem, *, alpha, nt):
    b = pl.program_id(0)
    i = pl.program_id(1)
    h = pl.program_id(2)
    tb = o_ref.shape[0]

    def x_copy():
        row0 = (b * nt + (nt - 1 - i)) * tb
        return pltpu.make_async_copy(x_hbm.at[pl.ds(row0, tb), :], o_ref, sem)

    @pl.when(i == 0)
    def _reset():
        s_ref[h] = jnp.zeros(s_ref.shape[1:], F32)

    @pl.when(h == 0)
    def _fetch_x():
        x_copy().start()

    def emit(rows, o_blocks):
        nvb = len(o_blocks)
        o = [of_ref[c, rows, :].astype(F32) + o_blocks[c] for c in range(nvb)]
        ss = sum(jnp.sum(oc * oc, axis=-1, keepdims=True) for oc in o)
        rstd = lax.rsqrt(ss * (1.0 / (nvb * COL_B)) + RMS_EPS)
        for c in range(nvb):
            cols = slice(c * COL_B, (c + 1) * COL_B)
            y_ref[rows, cols] = (o[c] * rstd * gn_ref[:, cols]
                                 * _silu(z_ref[c, rows, :].astype(F32))).astype(BF16)

    _gla_sweep_tile(q_ref, k_ref, v_ref, cum_ref, s_ref, h, emit, reverse=True)

    @pl.when(h == 0)
    def _start_acc():
        x_copy().wait()
        o_ref[...] = alpha * o_ref[...]

    o_ref[...] += _dot(y_ref[...], wo_ref[...])

    @pl.when(h == GLA_HEADS - 1)
    def _finish():
        for r in range(tb // CUM_ROWS):
            rows = slice(r * CUM_ROWS, (r + 1) * CUM_ROWS)
            o_ref[rows, :] = _layer_norm_rows(o_ref[rows, :], lng_ref[...], lnb_ref[...])


def _gla_bwd(hmain, cum_b, o_f, x2, p, lng, lnb, batch, alpha):
    d = x2.shape[1]
    w_out = p["w_out"]
    key_dim = p["key_dim"]
    d_inner = w_out.shape[0]
    rows, dk, dv, tb, nt, nvb, kk, kv, kz = _gla_dims(hmain, batch, key_dim, d_inner)
    grid = (batch, nt, GLA_HEADS)

    def row(b, i):
        return b * nt + (nt - 1 - i)

    kernel = functools.partial(_gla_bwd_kernel, alpha=alpha, nt=nt)
    return pl.pallas_call(
        kernel,
        grid=grid,
        in_specs=[
            pl.BlockSpec((None, tb, dk), lambda b, i, h: (h, row(b, i), 0)),
            pl.BlockSpec((None, tb, dk), lambda b, i, h: (kk + h, row(b, i), 0)),
            pl.BlockSpec((nvb, tb, COL_B), lambda b, i, h: (kv + h, row(b, i), 0)),
            pl.BlockSpec((None, tb, dk), lambda b, i, h: (h, row(b, i), 0)),
            pl.BlockSpec((nvb, tb, COL_B), lambda b, i, h: (h, row(b, i), 0)),
            pl.BlockSpec((nvb, tb, COL_B), lambda b, i, h: (kz + h, row(b, i), 0)),
            pl.BlockSpec(memory_space=pl.ANY),
            pl.BlockSpec((dv, d), lambda b, i, h: (h, 0)),
            pl.BlockSpec((1, dv), lambda b, i, h: (0, h)),
            pl.BlockSpec((1, d), lambda b, i, h: (0, 0)),
            pl.BlockSpec((1, d), lambda b, i, h: (0, 0)),
        ],
        out_specs=pl.BlockSpec((tb, d), lambda b, i, h: (row(b, i), 0)),
        out_shape=jax.ShapeDtypeStruct((rows, d), F32),
        scratch_shapes=[
            pltpu.VMEM((GLA_HEADS, nvb, dk, COL_B), F32),
            pltpu.VMEM((tb, dv), BF16),
            pltpu.SemaphoreType.DMA,
        ],
        compiler_params=pltpu.CompilerParams(
            dimension_semantics=("arbitrary", "arbitrary", "arbitrary"),
            vmem_limit_bytes=VMEM_LIMIT),
        name="gla_bwd",
    )(hmain, hmain, hmain, cum_b, o_f, hmain, x2, w_out, p["gn"], lng, lnb)


def _tri_blocks(reverse):
    r = jnp.arange(CUM_ROWS)[:, None]
    c = jnp.arange(CUM_ROWS)[None, :]
    same = (r // MACRO_B) == (c // MACRO_B)
    tri = (c >= r) if reverse else (c <= r)
    return (same & tri).astype(BF16)


def _prep_a(w_in, vg, vb, w_s, b_s, w_out):
    d_inner = w_out.shape[0]
    nc = d_inner // CW_A
    wc = _chunked_cols(w_in.astype(BF16), CW_A)
    d = wc.shape[1]
    v_pairs = wc[nc:2 * nc].reshape(nc // 2, 2, d, CW_A)
    uz = jnp.stack([wc[:nc], wc[2 * nc:]], axis=1)
    return dict(w1=jnp.concatenate([v_pairs, uz], axis=0), w_s=w_s.astype(BF16),
                b_s=b_s[:, :, None].astype(F32), w_out=w_out.astype(BF16),
                vg=vg[None, :], vb=vb[None, :])


def _prep_b(w_in, w_g2, b_g, gn_g, w_out):
    key_dim = w_g2.shape[2]
    d_inner = w_out.shape[0]
    n_main = 2 * key_dim + 2 * d_inner
    zeros = jnp.zeros((GATE_RANK, key_dim), F32)
    w_gate = jnp.concatenate([jnp.concatenate([w_g2[0], zeros], axis=1),
                              jnp.concatenate([zeros, w_g2[1]], axis=1)], axis=0)
    return dict(w_main=_chunked_cols(w_in[:, :n_main].astype(BF16), CW_P),
                w_gl=w_in[:, n_main:].astype(BF16),
                w_gate=w_gate.astype(BF16), b_gate=b_g.reshape(1, 2 * key_dim),
                gn=gn_g[None, :], w_out=w_out.astype(BF16), key_dim=key_dim, d_inner=d_inner,
                tri_f=_tri_blocks(False), tri_b=_tri_blocks(True))


def _mixer_b_layer(x2, batch, p, lng, lnb, alpha):
    hmain, cum_f, cum_b = _proj_b(x2, p)
    o_f = _gla_fwd(hmain, cum_f, batch, p["key_dim"], p["d_inner"])
    return _gla_bwd(hmain, cum_b, o_f, x2, p, lng, lnb, batch, alpha)


def _trunk(x, layers, ln_g, ln_b, alpha):
    batch, seq, d = x.shape
    x2 = x.reshape(batch * seq, d)
    for i, (kind, p) in enumerate(layers):
        lng, lnb = ln_g[i][None, :], ln_b[i][None, :]
        if kind == "a":
            x2 = _layer_a(x2, p, lng, lnb, alpha)
        else:
            x2 = _mixer_b_layer(x2, batch, p, lng, lnb, alpha)
    return x2.reshape(batch, seq, d)


def kernel(x_prompt, x_sample, w_in_a, ln_v_g_a, ln_v_b_a, w_s_a, b_s_a, w_out_a, w_in_b, w_g2_b, b_g_b, gn_g_b, w_out_b, ln_g, ln_b):
    depth = ln_g.shape[0]
    alpha = (2 * depth) ** 0.25
    layers = []
    for i in range(depth):
        j = i // 2
        if i % 2 == 0:
            layers.append(("a", _prep_a(w_in_a[j], ln_v_g_a[j], ln_v_b_a[j], w_s_a[j], b_s_a[j], w_out_a[j])))
        else:
            layers.append(("b", _prep_b(w_in_b[j], w_g2_b[j], b_g_b[j], gn_g_b[j], w_out_b[j])))
    y_prompt = _trunk(x_prompt, layers, ln_g, ln_b, alpha)
    y_sample = _trunk(x_sample, layers, ln_g, ln_b, alpha)
    return (y_prompt, y_sample)
```

```python
import functools

import jax
import jax.numpy as jnp
from jax import lax
from jax.experimental import pallas as pl
from jax.experimental.pallas import tpu as pltpu

F32 = jnp.float32
BF16 = jnp.bfloat16

CHUNK_A = 128
N_GROUPS_A = 16
GLA_HEADS = 4
GATE_RANK = 16
GATE_TAU = 16.0
SUB_B = 64
LN_EPS = 1e-5
RMS_EPS = 1e-6

TM_A = 1024
CW_A = 512
RB_A = 256
TM_P = 1024
CW_P = 1024
TB_G = 1024
MACRO_B = 256
CUM_ROWS = 256
COL_B = 256
VMEM_LIMIT = 60 * 1024 * 1024


def _dot(a, b):
    return jnp.dot(a, b, preferred_element_type=F32)


def _dot_nt(a, b):
    return lax.dot_general(a, b, (((1,), (1,)), ((), ())), preferred_element_type=F32)


def _dot_tn(a, b):
    return lax.dot_general(a, b, (((0,), (0,)), ((), ())), preferred_element_type=F32)


def _gelu_tanh(x):
    c = 0.7978845608028654
    return 0.5 * x * (1.0 + jnp.tanh(c * (x + 0.044715 * (x * x * x))))


def _silu(x):
    return 0.5 * x * (1.0 + jnp.tanh(0.5 * x))


def _log_sigmoid(x):
    return jnp.minimum(x, 0.0) - jnp.log(1.0 + jnp.exp(-jnp.abs(x)))


def _layer_norm_rows(h, g, b):
    mu = jnp.mean(h, axis=-1, keepdims=True)
    hc = h - mu
    var = jnp.mean(hc * hc, axis=-1, keepdims=True)
    return hc * lax.rsqrt(var + LN_EPS) * g + b


def _layer_a_kernel(x_hbm, w1_ref, wo_ref, ws_ref, bs_ref, vg_ref, vb_ref,
                    lng_ref, lnb_ref, o_ref,
                    stage_ref, xb_ref, gv_ref, sum_ref, sq_ref, mu_ref, rstd_ref, y_ref, sem,
                    *, nc, d_inner, alpha, n_tiles):
    i = pl.program_id(0)
    j = pl.program_id(1)
    tm = o_ref.shape[0]
    cw = w1_ref.shape[2]
    gdim = d_inner // N_GROUPS_A
    npv = nc // 2
    lanes = sum_ref.shape[1]

    def x_copy(tile):
        return pltpu.make_async_copy(x_hbm.at[pl.ds(tile * tm, tm), :], stage_ref, sem)

    @pl.when((j == 0) & (i == 0))
    def _first_fetch():
        x_copy(0).start()

    @pl.when(j == 0)
    def _init():
        x_copy(i).wait()
        for r in range(tm // RB_A):
            rows = slice(r * RB_A, (r + 1) * RB_A)
            x = stage_ref[rows, :]
            xb_ref[rows, :] = x.astype(BF16)
            o_ref[rows, :] = alpha * x
        sum_ref[...] = jnp.zeros_like(sum_ref)
        sq_ref[...] = jnp.zeros_like(sq_ref)

    @pl.when((j == 1) & (i + 1 < n_tiles))
    def _prefetch():
        x_copy(i + 1).start()

    @pl.when(j < npv)
    def _phase_v():
        for t in range(2):
            gv = _gelu_tanh(_dot(xb_ref[...], w1_ref[t]))
            gv_ref[2 * j + t] = gv.astype(BF16)
            part = gv[:, 0:lanes]
            part2 = part * part
            for l in range(1, cw // lanes):
                blk = gv[:, l * lanes:(l + 1) * lanes]
                part = part + blk
                part2 = part2 + blk * blk
            sum_ref[...] += part
            sq_ref[...] += part2

    @pl.when(j == npv)
    def _start_acc():
        mu = jnp.sum(sum_ref[...], axis=1, keepdims=True) * (1.0 / d_inner)
        var = jnp.sum(sq_ref[...], axis=1, keepdims=True) * (1.0 / d_inner) - mu * mu
        mu_ref[...] = mu
        rstd_ref[...] = lax.rsqrt(var + LN_EPS)

    @pl.when(j >= npv)
    def _phase_mix():
        c = j - npv
        xb = xb_ref[...]
        gate = _gelu_tanh(_dot(xb, w1_ref[0])) * _silu(_dot(xb, w1_ref[1]))
        vn = ((gv_ref[c].astype(F32) - mu_ref[...]) * rstd_ref[...] * vg_ref[...]
              + vb_ref[...]).astype(BF16)
        for g in range(cw // gdim):
            cols = slice(g * gdim, (g + 1) * gdim)
            wsg = ws_ref[g]
            bsg = jnp.broadcast_to(bs_ref[g], (CHUNK_A, gdim))
            for n in range(tm // CHUNK_A):
                rows = slice(n * CHUNK_A, (n + 1) * CHUNK_A)
                s = _dot(wsg, vn[rows, cols]) + bsg
                y_ref[rows, cols] = (gate[rows, cols] * s).astype(BF16)
        o_ref[...] += _dot(y_ref[...], wo_ref[...])

    @pl.when(j == npv + nc - 1)
    def _finish():
        for r in range(tm // RB_A):
            rows = slice(r * RB_A, (r + 1) * RB_A)
            o_ref[rows, :] = _layer_norm_rows(o_ref[rows, :], lng_ref[...], lnb_ref[...])


def _layer_a(x2, p, lng, lnb, alpha):
    rows, d = x2.shape
    w1, w_out = p["w1"], p["w_out"]
    d_inner = w_out.shape[0]
    cw = w1.shape[3]
    nc = d_inner // cw
    npv = nc // 2
    gpc = cw // (d_inner // N_GROUPS_A)
    tm = min(TM_A, rows)
    grid = (rows // tm, npv + nc)

    def cmix(j):
        return jnp.maximum(j - npv, 0)

    kernel = functools.partial(_layer_a_kernel, nc=nc, d_inner=d_inner, alpha=alpha, n_tiles=grid[0])
    return pl.pallas_call(
        kernel,
        grid=grid,
        in_specs=[
            pl.BlockSpec(memory_space=pl.ANY),
            pl.BlockSpec((None, 2, d, cw), lambda i, j: (j, 0, 0, 0)),
            pl.BlockSpec((cw, d), lambda i, j: (cmix(j), 0)),
            pl.BlockSpec((gpc, CHUNK_A, CHUNK_A), lambda i, j: (cmix(j), 0, 0)),
            pl.BlockSpec((gpc, CHUNK_A, 1), lambda i, j: (cmix(j), 0, 0)),
            pl.BlockSpec((1, cw), lambda i, j: (0, cmix(j))),
            pl.BlockSpec((1, cw), lambda i, j: (0, cmix(j))),
            pl.BlockSpec((1, d), lambda i, j: (0, 0)),
            pl.BlockSpec((1, d), lambda i, j: (0, 0)),
        ],
        out_specs=pl.BlockSpec((tm, d), lambda i, j: (i, 0)),
        out_shape=jax.ShapeDtypeStruct((rows, d), F32),
        scratch_shapes=[
            pltpu.VMEM((tm, d), F32),
            pltpu.VMEM((tm, d), BF16),
            pltpu.VMEM((nc, tm, cw), BF16),
            pltpu.VMEM((tm, 128), F32),
            pltpu.VMEM((tm, 128), F32),
            pltpu.VMEM((tm, 1), F32),
            pltpu.VMEM((tm, 1), F32),
            pltpu.VMEM((tm, cw), BF16),
            pltpu.SemaphoreType.DMA,
        ],
        compiler_params=pltpu.CompilerParams(
            dimension_semantics=("arbitrary", "arbitrary"),
            vmem_limit_bytes=VMEM_LIMIT),
        name="layer_a",
    )(x2, w1, w_out, p["w_s"], p["b_s"], p["vg"], p["vb"], lng, lnb)


def _proj_b_kernel(x_ref, w_ref, wgl_ref, wg_ref, bg_ref, lf_ref, lb_ref,
                   h_ref, gf_ref, rb_ref, xb_ref, *, key_dim):
    j = pl.program_id(1)
    tm = x_ref.shape[0]

    @pl.when(j == 0)
    def _gates():
        for r in range(tm // CUM_ROWS):
            rows = slice(r * CUM_ROWS, (r + 1) * CUM_ROWS)
            xb = x_ref[rows, :].astype(BF16)
            xb_ref[rows, :] = xb
            gl = _dot(xb, wgl_ref[...])
            pre = _dot(gl.astype(BF16), wg_ref[...]) + bg_ref[...]
            glog = _log_sigmoid(pre) * (1.0 / GATE_TAU)
            for dst, tri, cols in ((gf_ref, lf_ref, slice(0, key_dim)),
                                   (rb_ref, lb_ref, slice(key_dim, 2 * key_dim))):
                gb = glog[:, cols]
                hi = gb.astype(BF16)
                lo = (gb - hi.astype(F32)).astype(BF16)
                cs = _dot(tri[...], hi) + _dot(tri[...], lo)
                for c in range(key_dim // COL_B):
                    dst[c, rows, :] = cs[:, c * COL_B:(c + 1) * COL_B]

    res = _dot(xb_ref[...], w_ref[...])
    for c in range(res.shape[1] // COL_B):
        h_ref[c] = res[:, c * COL_B:(c + 1) * COL_B].astype(BF16)


def _proj_b(x2, p):
    rows, d = x2.shape
    w_main = p["w_main"]
    n_chunks, _, cw = w_main.shape
    key_dim = p["key_dim"]
    tm = min(TM_P, rows)
    grid = (rows // tm, n_chunks)
    kernel = functools.partial(_proj_b_kernel, key_dim=key_dim)
    small = [p["w_gl"], p["w_gate"], p["b_gate"], p["tri_f"], p["tri_b"]]
    return pl.pallas_call(
        kernel,
        grid=grid,
        in_specs=[
            pl.BlockSpec((tm, d), lambda i, j: (i, 0)),
            pl.BlockSpec((None, d, cw), lambda i, j: (j, 0, 0)),
        ] + [pl.BlockSpec(a.shape, lambda i, j: (0, 0)) for a in small],
        out_specs=[
            pl.BlockSpec((cw // COL_B, tm, COL_B), lambda i, j: (j, i, 0)),
            pl.BlockSpec((key_dim // COL_B, tm, COL_B), lambda i, j: (0, i, 0)),
            pl.BlockSpec((key_dim // COL_B, tm, COL_B), lambda i, j: (0, i, 0)),
        ],
        out_shape=[
            jax.ShapeDtypeStruct((n_chunks * cw // COL_B, rows, COL_B), BF16),
            jax.ShapeDtypeStruct((key_dim // COL_B, rows, COL_B), F32),
            jax.ShapeDtypeStruct((key_dim // COL_B, rows, COL_B), F32),
        ],
        scratch_shapes=[pltpu.VMEM((tm, d), BF16)],
        compiler_params=pltpu.CompilerParams(
            dimension_semantics=("arbitrary", "arbitrary"),
            vmem_limit_bytes=VMEM_LIMIT),
        name="proj_b",
    )(x2, w_main, *small)


def _row_to_col(vec_row, width):
    n = vec_row.shape[1]
    return jnp.broadcast_to(vec_row, (width, n)).T


def _gla_macro_chunk(q, k, v_blocks, cum, state_blocks, reverse):
    c_rows, dk = q.shape
    ns = c_rows // SUB_B
    qs = q.astype(F32) * (dk ** -0.5)
    kf = k.astype(F32)
    order = list(range(ns))[::-1] if reverse else list(range(ns))
    mid_row = SUB_B // 2 - 1 if reverse else SUB_B // 2
    end_row = 0 if reverse else SUB_B - 1

    rsl = {a: slice(a * SUB_B, (a + 1) * SUB_B) for a in range(ns)}
    r_end, q_diag, k_diag, q_off, k_off = {}, {}, {}, {}, {}
    r_prev = {}
    zero_row = jnp.zeros((1, dk), F32)
    prev = zero_row
    for a in order:
        ca = cum[rsl[a]]
        mid = ca[mid_row:mid_row + 1]
        r_end[a] = ca[end_row:end_row + 1]
        r_prev[a] = prev
        q_diag[a] = (qs[rsl[a]] * jnp.exp(ca - mid)).astype(BF16)
        k_diag[a] = (kf[rsl[a]] * jnp.exp(mid - ca)).astype(BF16)
        q_off[a] = qs[rsl[a]] * jnp.exp(ca - prev)
        k_off[a] = kf[rsl[a]] * jnp.exp(r_end[a] - ca)
        prev = r_end[a]
    r_last = prev

    ri = lax.broadcasted_iota(jnp.int32, (SUB_B, SUB_B), 0)
    ci = lax.broadcasted_iota(jnp.int32, (SUB_B, SUB_B), 1)
    keep = (ci > ri) if reverse else (ci <= ri)

    k_off_b = {a: k_off[a].astype(BF16) for a in range(ns)}
    a_rows = []
    q_state = []
    for a in range(ns):
        blocks = []
        for b in range(ns):
            if b == a:
                blk = jnp.where(keep, _dot_nt(q_diag[a], k_diag[a]), 0.0)
            elif order.index(b) < order.index(a):
                if order.index(b) + 1 == order.index(a):
                    qa = q_off[a]
                else:
                    qa = q_off[a] * jnp.exp(r_prev[a] - r_end[b])
                blk = _dot_nt(qa.astype(BF16), k_off_b[b])
            else:
                blk = jnp.zeros((SUB_B, SUB_B), F32)
            blocks.append(blk)
        a_rows.append(jnp.concatenate(blocks, axis=1) if ns > 1 else blocks[0])
        q_state.append(q_off[a] * jnp.exp(r_prev[a]))
    a_full = jnp.concatenate(a_rows, axis=0) if ns > 1 else a_rows[0]
    q_st = jnp.concatenate(q_state, axis=0) if ns > 1 else q_state[0]

    a_b = a_full.astype(BF16)
    q_b = q_st.astype(BF16)
    k_dec = [k_off[a] * jnp.exp(r_last - r_end[a]) for a in range(ns)]
    k_dec = (jnp.concatenate(k_dec, axis=0) if ns > 1 else k_dec[0]).astype(BF16)
    dcol = _row_to_col(jnp.exp(r_last), 128)
    dcol = jnp.concatenate([dcol] * (v_blocks[0].shape[1] // 128), axis=1)
    o_blocks, new_state = [], []
    for vc, sc in zip(v_blocks, state_blocks):
        o_blocks.append(_dot(a_b, vc) + _dot(q_b, sc.astype(BF16)))
        new_state.append(sc * dcol + _dot_tn(k_dec, vc))
    return o_blocks, new_state


def _gla_sweep_tile(q_ref, k_ref, v_ref, cum_ref, s_ref, h, emit, reverse):
    tb = q_ref.shape[0]
    nvb = v_ref.shape[0]
    nm = tb // MACRO_B
    chunks = range(nm - 1, -1, -1) if reverse else range(nm)
    for m in chunks:
        rows = slice(m * MACRO_B, (m + 1) * MACRO_B)
        o_blocks, s_new = _gla_macro_chunk(
            q_ref[rows, :], k_ref[rows, :], [v_ref[c, rows, :] for c in range(nvb)], cum_ref[rows, :],
            [s_ref[h, c] for c in range(nvb)], reverse)
        for c in range(nvb):
            s_ref[h, c] = s_new[c]
        emit(rows, o_blocks)


def _gla_fwd_kernel(q_ref, k_ref, v_ref, cum_ref, o_ref, s_ref):
    i = pl.program_id(1)
    h = pl.program_id(2)

    @pl.when(i == 0)
    def _reset():
        s_ref[h] = jnp.zeros(s_ref.shape[1:], F32)

    def emit(rows, o_blocks):
        for c, o in enumerate(o_blocks):
            o_ref[c, rows, :] = o.astype(o_ref.dtype)

    _gla_sweep_tile(q_ref, k_ref, v_ref, cum_ref, s_ref, h, emit, reverse=False)


def _gla_dims(hmain, batch, key_dim, d_inner):
    rows = hmain.shape[1]
    dk = key_dim // GLA_HEADS
    dv = d_inner // GLA_HEADS
    assert dk == COL_B and dv % COL_B == 0
    tb = min(TB_G, rows // batch)
    nt = rows // batch // tb
    nvb = dv // COL_B
    kk = key_dim // COL_B
    kv = 2 * key_dim // (nvb * COL_B)
    kz = (2 * key_dim + d_inner) // (nvb * COL_B)
    return rows, dk, dv, tb, nt, nvb, kk, kv, kz


def _gla_fwd(hmain, cum_f, batch, key_dim, d_inner):
    rows, dk, dv, tb, nt, nvb, kk, kv, _ = _gla_dims(hmain, batch, key_dim, d_inner)
    grid = (batch, nt, GLA_HEADS)
    return pl.pallas_call(
        _gla_fwd_kernel,
        grid=grid,
        in_specs=[
            pl.BlockSpec((None, tb, dk), lambda b, i, h: (h, b * nt + i, 0)),
            pl.BlockSpec((None, tb, dk), lambda b, i, h: (kk + h, b * nt + i, 0)),
            pl.BlockSpec((nvb, tb, COL_B), lambda b, i, h: (kv + h, b * nt + i, 0)),
            pl.BlockSpec((None, tb, dk), lambda b, i, h: (h, b * nt + i, 0)),
        ],
        out_specs=pl.BlockSpec((nvb, tb, COL_B), lambda b, i, h: (h, b * nt + i, 0)),
        out_shape=jax.ShapeDtypeStruct((GLA_HEADS * nvb, rows, COL_B), BF16),
        scratch_shapes=[pltpu.VMEM((GLA_HEADS, nvb, dk, COL_B), F32)],
        compiler_params=pltpu.CompilerParams(
            dimension_semantics=("arbitrary", "arbitrary", "arbitrary"),
            vmem_limit_bytes=VMEM_LIMIT),
        name="gla_fwd",
    )(hmain, hmain, hmain, cum_f)


def _gla_bwd_kernel(q_ref, k_ref, v_ref, cum_ref, of_ref, z_ref, x_ref, wo_ref, gn_ref,
                    lng_ref, lnb_ref, o_ref, s_ref, y_ref, *, alpha):
    i = pl.program_id(1)
    h = pl.program_id(2)
    tb = o_ref.shape[0]
    xr = x_ref.shape[0]

    @pl.when(i == 0)
    def _reset():
        s_ref[h] = jnp.zeros(s_ref.shape[1:], F32)

    @pl.when(h == 0)
    def _zero_acc():
        o_ref[...] = jnp.zeros_like(o_ref)

    def emit(rows, o_blocks):
        nvb = len(o_blocks)
        o = [of_ref[c, rows, :].astype(F32) + o_blocks[c] for c in range(nvb)]
        ss = sum(jnp.sum(oc * oc, axis=-1, keepdims=True) for oc in o)
        rstd = lax.rsqrt(ss * (1.0 / (nvb * COL_B)) + RMS_EPS)
        for c in range(nvb):
            cols = slice(c * COL_B, (c + 1) * COL_B)
            y_ref[rows, cols] = (o[c] * rstd * gn_ref[:, cols]
                                 * _silu(z_ref[c, rows, :].astype(F32))).astype(BF16)

    _gla_sweep_tile(q_ref, k_ref, v_ref, cum_ref, s_ref, h, emit, reverse=True)

    o_ref[...] += _dot(y_ref[...], wo_ref[...])
    xrows = pl.ds(pl.multiple_of(h * xr, xr), xr)
    o_ref[xrows, :] += alpha * x_ref[...]

    @pl.when(h == GLA_HEADS - 1)
    def _finish():
        for r in range(tb // CUM_ROWS):
            rows = slice(r * CUM_ROWS, (r + 1) * CUM_ROWS)
            o_ref[rows, :] = _layer_norm_rows(o_ref[rows, :], lng_ref[...], lnb_ref[...])


def _gla_bwd(hmain, cum_b, o_f, x2, p, lng, lnb, batch, alpha):
    d = x2.shape[1]
    w_out = p["w_out"]
    key_dim = p["key_dim"]
    d_inner = w_out.shape[0]
    rows, dk, dv, tb, nt, nvb, kk, kv, kz = _gla_dims(hmain, batch, key_dim, d_inner)
    grid = (batch, nt, GLA_HEADS)

    def row(b, i):
        return b * nt + (nt - 1 - i)

    xr = tb // GLA_HEADS
    kernel = functools.partial(_gla_bwd_kernel, alpha=alpha)
    return pl.pallas_call(
        kernel,
        grid=grid,
        in_specs=[
            pl.BlockSpec((None, tb, dk), lambda b, i, h: (h, row(b, i), 0)),
            pl.BlockSpec((None, tb, dk), lambda b, i, h: (kk + h, row(b, i), 0)),
            pl.BlockSpec((nvb, tb, COL_B), lambda b, i, h: (kv + h, row(b, i), 0)),
            pl.BlockSpec((None, tb, dk), lambda b, i, h: (h, row(b, i), 0)),
            pl.BlockSpec((nvb, tb, COL_B), lambda b, i, h: (h, row(b, i), 0)),
            pl.BlockSpec((nvb, tb, COL_B), lambda b, i, h: (kz + h, row(b, i), 0)),
            pl.BlockSpec((xr, d), lambda b, i, h: (row(b, i) * GLA_HEADS + h, 0)),
            pl.BlockSpec((dv, d), lambda b, i, h: (h, 0)),
            pl.BlockSpec((1, dv), lambda b, i, h: (0, h)),
            pl.BlockSpec((1, d), lambda b, i, h: (0, 0)),
            pl.BlockSpec((1, d), lambda b, i, h: (0, 0)),
        ],
        out_specs=pl.BlockSpec((tb, d), lambda b, i, h: (row(b, i), 0)),
        out_shape=jax.ShapeDtypeStruct((rows, d), F32),
        scratch_shapes=[
            pltpu.VMEM((GLA_HEADS, nvb, dk, COL_B), F32),
            pltpu.VMEM((tb, dv), BF16),
        ],
        compiler_params=pltpu.CompilerParams(
            dimension_semantics=("arbitrary", "arbitrary", "arbitrary"),
            vmem_limit_bytes=VMEM_LIMIT),
        name="gla_bwd",
    )(hmain, hmain, hmain, cum_b, o_f, hmain, x2, w_out, p["gn"], lng, lnb)


def _tri_blocks(reverse):
    r = jnp.arange(CUM_ROWS)[:, None]
    c = jnp.arange(CUM_ROWS)[None, :]
    same = (r // MACRO_B) == (c // MACRO_B)
    tri = (c >= r) if reverse else (c <= r)
    return (same & tri).astype(BF16)


def _cast_block_kernel(w_ref, o_ref):
    o_ref[...] = w_ref[...].astype(BF16)


def _relayout_cols(w3, layer, cw, n_slots, src_chunk):
    k = w3.shape[1]
    return pl.pallas_call(
        _cast_block_kernel,
        grid=(n_slots,),
        in_specs=[pl.BlockSpec((None, k, cw), lambda s: (layer, 0, src_chunk(s)))],
        out_specs=pl.BlockSpec((None, k, cw), lambda s: (s, 0, 0)),
        out_shape=jax.ShapeDtypeStruct((n_slots, k, cw), BF16),
        compiler_params=pltpu.CompilerParams(dimension_semantics=("arbitrary",)),
        name="weight_relayout",
    )(w3)


def _prep_a(w_in3, layer, vg, vb, w_s, b_s, w_out):
    d = w_in3.shape[1]
    d_inner = w_out.shape[0]
    nc = d_inner // CW_A
    npv = nc // 2

    def src_chunk(s):
        pair, t = s // 2, s % 2
        return jnp.where(pair < npv, nc + s, jnp.where(t == 0, pair - npv, 2 * nc + pair - npv))

    w1 = _relayout_cols(w_in3, layer, CW_A, 3 * nc, src_chunk).reshape(npv + nc, 2, d, CW_A)
    return dict(w1=w1, w_s=w_s.astype(BF16),
                b_s=b_s[:, :, None].astype(F32), w_out=w_out.astype(BF16),
                vg=vg[None, :], vb=vb[None, :])


def _prep_b(w_in3, layer, w_g2, b_g, gn_g, w_out):
    key_dim = w_g2.shape[2]
    d_inner = w_out.shape[0]
    n_main = 2 * key_dim + 2 * d_inner
    zeros = jnp.zeros((GATE_RANK, key_dim), F32)
    w_gate = jnp.concatenate([jnp.concatenate([w_g2[0], zeros], axis=1),
                              jnp.concatenate([zeros, w_g2[1]], axis=1)], axis=0)
    return dict(w_main=_relayout_cols(w_in3, layer, CW_P, n_main // CW_P, lambda s: s),
                w_gl=w_in3[layer, :, n_main:].astype(BF16),
                w_gate=w_gate.astype(BF16), b_gate=b_g.reshape(1, 2 * key_dim),
                gn=gn_g[None, :], w_out=w_out.astype(BF16), key_dim=key_dim, d_inner=d_inner,
                tri_f=_tri_blocks(False), tri_b=_tri_blocks(True))


def _mixer_b_layer(x2, batch, p, lng, lnb, alpha):
    hmain, cum_f, cum_b = _proj_b(x2, p)
    o_f = _gla_fwd(hmain, cum_f, batch, p["key_dim"], p["d_inner"])
    return _gla_bwd(hmain, cum_b, o_f, x2, p, lng, lnb, batch, alpha)


def _trunk(x, layers, ln_g, ln_b, alpha):
    batch, seq, d = x.shape
    x2 = x.reshape(batch * seq, d)
    for i, (kind, p) in enumerate(layers):
        lng, lnb = ln_g[i][None, :], ln_b[i][None, :]
        if kind == "a":
            x2 = _layer_a(x2, p, lng, lnb, alpha)
        else:
            x2 = _mixer_b_layer(x2, batch, p, lng, lnb, alpha)
    return x2.reshape(batch, seq, d)


def kernel(x_prompt, x_sample, w_in_a, ln_v_g_a, ln_v_b_a, w_s_a, b_s_a, w_out_a, w_in_b, w_g2_b, b_g_b, gn_g_b, w_out_b, ln_g, ln_b):
    depth = ln_g.shape[0]
    alpha = (2 * depth) ** 0.25
    layers = []
    for i in range(depth):
        j = i // 2
        if i % 2 == 0:
            layers.append(("a", _prep_a(w_in_a, j, ln_v_g_a[j], ln_v_b_a[j], w_s_a[j], b_s_a[j], w_out_a[j])))
        else:
            layers.append(("b", _prep_b(w_in_b, j, w_g2_b[j], b_g_b[j], gn_g_b[j], w_out_b[j])))
    y_prompt = _trunk(x_prompt, layers, ln_g, ln_b, alpha)
    y_sample = _trunk(x_sample, layers, ln_g, ln_b, alpha)
    return (y_prompt, y_sample)
```

```python
import functools

import jax
import jax.numpy as jnp
from jax import lax
from jax.experimental import pallas as pl
from jax.experimental.pallas import tpu as pltpu

F32 = jnp.float32
BF16 = jnp.bfloat16

CHUNK_A = 128
N_GROUPS_A = 16
GLA_HEADS = 4
GATE_RANK = 16
GATE_TAU = 16.0
LOG2_E = 1.4426950408889634
SUB_B = 64
LN_EPS = 1e-5
RMS_EPS = 1e-6

TM_A = 1024
CW_A = 512
RB_A = 256
MXU_COLS = 256
TM_P = 1024
CW_P = 1024
TB_G = 1024
MACRO_B = 256
CUM_ROWS = 256
COL_B = 256
VMEM_LIMIT = 60 * 1024 * 1024


def _dot(a, b):
    return jnp.dot(a, b, preferred_element_type=F32)


def _dot_nt(a, b):
    return lax.dot_general(a, b, (((1,), (1,)), ((), ())), preferred_element_type=F32)


def _dot_tn(a, b):
    return lax.dot_general(a, b, (((0,), (0,)), ((), ())), preferred_element_type=F32)


def _gelu_tanh(x):
    c = 0.7978845608028654
    return 0.5 * x * (1.0 + jnp.tanh(c * (x + 0.044715 * (x * x * x))))


def _silu(x):
    return 0.5 * x * (1.0 + jnp.tanh(0.5 * x))


def _log_sigmoid(x):
    return jnp.minimum(x, 0.0) - jnp.log(1.0 + jnp.exp(-jnp.abs(x)))


def _layer_norm_rows(h, g, b):
    mu = jnp.mean(h, axis=-1, keepdims=True)
    hc = h - mu
    var = jnp.mean(hc * hc, axis=-1, keepdims=True)
    return hc * lax.rsqrt(var + LN_EPS) * g + b


def _layer_a_kernel(x_hbm, w1_ref, wo_ref, ws_ref, bs_ref, vg_ref, vb_ref,
                    lng_ref, lnb_ref, o_ref,
                    stage_ref, xb_ref, gv_ref, sum_ref, sq_ref, mu_ref, rstd_ref, y_ref, sem,
                    *, nc, d_inner, alpha, n_tiles):
    i = pl.program_id(0)
    j = pl.program_id(1)
    tm = o_ref.shape[0]
    cw = w1_ref.shape[2]
    gdim = d_inner // N_GROUPS_A
    npv = nc // 2
    lanes = sum_ref.shape[1]

    def x_copy(tile):
        return pltpu.make_async_copy(x_hbm.at[pl.ds(tile * tm, tm), :], stage_ref, sem)

    @pl.when((j == 0) & (i == 0))
    def _first_fetch():
        x_copy(0).start()

    def phase_v_body():
        for t in range(2):
            for pc in range(cw // MXU_COLS):
                cols = slice(pc * MXU_COLS, (pc + 1) * MXU_COLS)
                gv = _gelu_tanh(_dot(xb_ref[...], w1_ref[t, :, cols]))
                gv_ref[2 * j + t, :, cols] = gv.astype(BF16)
                part = gv[:, 0:lanes]
                part2 = part * part
                for l in range(1, MXU_COLS // lanes):
                    blk = gv[:, l * lanes:(l + 1) * lanes]
                    part = part + blk
                    part2 = part2 + blk * blk
                sum_ref[...] += part
                sq_ref[...] += part2

    @pl.when(j == 0)
    def _first_step():
        x_copy(i).wait()
        for r in range(tm // RB_A):
            rows = slice(r * RB_A, (r + 1) * RB_A)
            x = stage_ref[rows, :]
            xb_ref[rows, :] = x.astype(BF16)
            o_ref[rows, :] = alpha * x
        sum_ref[...] = jnp.zeros_like(sum_ref)
        sq_ref[...] = jnp.zeros_like(sq_ref)
        phase_v_body()

    @pl.when((j == 1) & (i + 1 < n_tiles))
    def _prefetch():
        x_copy(i + 1).start()

    @pl.when((j > 0) & (j < npv))
    def _phase_v():
        phase_v_body()

    @pl.when(j == npv)
    def _start_acc():
        mu = jnp.sum(sum_ref[...], axis=1, keepdims=True) * (1.0 / d_inner)
        var = jnp.sum(sq_ref[...], axis=1, keepdims=True) * (1.0 / d_inner) - mu * mu
        mu_ref[...] = mu
        rstd_ref[...] = lax.rsqrt(var + LN_EPS)

    @pl.when(j >= npv)
    def _phase_mix():
        c = j - npv
        xb = xb_ref[...]
        gate = _gelu_tanh(_dot(xb, w1_ref[0])) * _silu(_dot(xb, w1_ref[1]))
        vn = ((gv_ref[c].astype(F32) - mu_ref[...]) * rstd_ref[...] * vg_ref[...]
              + vb_ref[...]).astype(BF16)
        for g in range(cw // gdim):
            cols = slice(g * gdim, (g + 1) * gdim)
            wsg = ws_ref[g]
            bsg = jnp.broadcast_to(bs_ref[g], (CHUNK_A, gdim))
            for n in range(tm // CHUNK_A):
                rows = slice(n * CHUNK_A, (n + 1) * CHUNK_A)
                s = _dot(wsg, vn[rows, cols]) + bsg
                y_ref[rows, cols] = (gate[rows, cols] * s).astype(BF16)
        o_ref[...] += _dot(y_ref[...], wo_ref[...])

    @pl.when(j == npv + nc - 1)
    def _finish():
        for r in range(tm // RB_A):
            rows = slice(r * RB_A, (r + 1) * RB_A)
            o_ref[rows, :] = _layer_norm_rows(o_ref[rows, :], lng_ref[...], lnb_ref[...])


def _layer_a(x2, p, lng, lnb, alpha):
    rows, d = x2.shape
    w1, w_out = p["w1"], p["w_out"]
    d_inner = w_out.shape[0]
    cw = w1.shape[3]
    nc = d_inner // cw
    npv = nc // 2
    gpc = cw // (d_inner // N_GROUPS_A)
    tm = min(TM_A, rows)
    grid = (rows // tm, npv + nc)

    def cmix(j):
        return jnp.maximum(j - npv, 0)

    kernel = functools.partial(_layer_a_kernel, nc=nc, d_inner=d_inner, alpha=alpha, n_tiles=grid[0])
    return pl.pallas_call(
        kernel,
        grid=grid,
        in_specs=[
            pl.BlockSpec(memory_space=pl.ANY),
            pl.BlockSpec((None, 2, d, cw), lambda i, j: (j, 0, 0, 0)),
            pl.BlockSpec((cw, d), lambda i, j: (cmix(j), 0)),
            pl.BlockSpec((gpc, CHUNK_A, CHUNK_A), lambda i, j: (cmix(j), 0, 0)),
            pl.BlockSpec((gpc, CHUNK_A, 1), lambda i, j: (cmix(j), 0, 0)),
            pl.BlockSpec((1, cw), lambda i, j: (0, cmix(j))),
            pl.BlockSpec((1, cw), lambda i, j: (0, cmix(j))),
            pl.BlockSpec((1, d), lambda i, j: (0, 0)),
            pl.BlockSpec((1, d), lambda i, j: (0, 0)),
        ],
        out_specs=pl.BlockSpec((tm, d), lambda i, j: (i, 0)),
        out_shape=jax.ShapeDtypeStruct((rows, d), F32),
        scratch_shapes=[
            pltpu.VMEM((tm, d), F32),
            pltpu.VMEM((tm, d), BF16),
            pltpu.VMEM((nc, tm, cw), BF16),
            pltpu.VMEM((tm, 128), F32),
            pltpu.VMEM((tm, 128), F32),
            pltpu.VMEM((tm, 1), F32),
            pltpu.VMEM((tm, 1), F32),
            pltpu.VMEM((tm, cw), BF16),
            pltpu.SemaphoreType.DMA,
        ],
        compiler_params=pltpu.CompilerParams(
            dimension_semantics=("arbitrary", "arbitrary"),
            vmem_limit_bytes=VMEM_LIMIT),
        name="layer_a",
    )(x2, w1, w_out, p["w_s"], p["b_s"], p["vg"], p["vb"], lng, lnb)


def _proj_b_kernel(x_ref, w_ref, wgl_ref, wg_ref, bg_ref, lf_ref, lb_ref,
                   h_ref, gf_ref, rb_ref, xb_ref, gl_ref, *, key_dim):
    j = pl.program_id(1)
    tm = x_ref.shape[0]

    def project():
        res = _dot(xb_ref[...], w_ref[...])
        for c in range(res.shape[1] // COL_B):
            h_ref[c] = res[:, c * COL_B:(c + 1) * COL_B].astype(BF16)

    def gates(r):
        rows = slice(r * CUM_ROWS, (r + 1) * CUM_ROWS)
        pre = _dot(gl_ref[rows, :], wg_ref[...]) + bg_ref[...]
        glog = _log_sigmoid(pre) * (LOG2_E / GATE_TAU)
        for dst, tri, cols in ((gf_ref, lf_ref, slice(0, key_dim)),
                               (rb_ref, lb_ref, slice(key_dim, 2 * key_dim))):
            gb = glog[:, cols]
            hi = gb.astype(BF16)
            lo = (gb - hi.astype(F32)).astype(BF16)
            cs = _dot(tri[...], hi) + _dot(tri[...], lo)
            for c in range(key_dim // COL_B):
                dst[c, rows, :] = cs[:, c * COL_B:(c + 1) * COL_B]

    @pl.when(j == 0)
    def _gates_and_first_chunk():
        for r in range(tm // CUM_ROWS):
            rows = slice(r * CUM_ROWS, (r + 1) * CUM_ROWS)
            xb_ref[rows, :] = x_ref[rows, :].astype(BF16)
        gl_t = _dot_nt(wgl_ref[...], xb_ref[...])
        gl_ref[...] = gl_t.T.astype(BF16)
        for r in range(tm // CUM_ROWS):
            gates(r)
        project()

    @pl.when(j > 0)
    def _other_chunks():
        project()


def _proj_b(x2, p):
    rows, d = x2.shape
    w_main = p["w_main"]
    n_chunks, _, cw = w_main.shape
    key_dim = p["key_dim"]
    tm = min(TM_P, rows)
    grid = (rows // tm, n_chunks)
    kernel = functools.partial(_proj_b_kernel, key_dim=key_dim)
    small = [p["w_gl"], p["w_gate"], p["b_gate"], p["tri_f"], p["tri_b"]]
    return pl.pallas_call(
        kernel,
        grid=grid,
        in_specs=[
            pl.BlockSpec((tm, d), lambda i, j: (i, 0)),
            pl.BlockSpec((None, d, cw), lambda i, j: (j, 0, 0)),
        ] + [pl.BlockSpec(a.shape, lambda i, j: (0, 0)) for a in small],
        out_specs=[
            pl.BlockSpec((cw // COL_B, tm, COL_B), lambda i, j: (j, i, 0)),
            pl.BlockSpec((key_dim // COL_B, tm, COL_B), lambda i, j: (0, i, 0)),
            pl.BlockSpec((key_dim // COL_B, tm, COL_B), lambda i, j: (0, i, 0)),
        ],
        out_shape=[
            jax.ShapeDtypeStruct((n_chunks * cw // COL_B, rows, COL_B), BF16),
            jax.ShapeDtypeStruct((key_dim // COL_B, rows, COL_B), F32),
            jax.ShapeDtypeStruct((key_dim // COL_B, rows, COL_B), F32),
        ],
        scratch_shapes=[pltpu.VMEM((tm, d), BF16),
                        pltpu.VMEM((tm, p["w_gl"].shape[0]), BF16)],
        compiler_params=pltpu.CompilerParams(
            dimension_semantics=("arbitrary", "arbitrary"),
            vmem_limit_bytes=VMEM_LIMIT),
        name="proj_b",
    )(x2, w_main, *small)


def _row_to_col(vec_row, width):
    n = vec_row.shape[1]
    return jnp.broadcast_to(vec_row, (width, n)).T


def _gla_macro_chunk(q, k, v_blocks, cum, state_blocks, reverse):
    c_rows, dk = q.shape
    ns = c_rows // SUB_B
    qs = q.astype(F32) * (dk ** -0.5)
    kf = k.astype(F32)
    order = list(range(ns))[::-1] if reverse else list(range(ns))
    mid_row = SUB_B // 2 - 1 if reverse else SUB_B // 2
    end_row = 0 if reverse else SUB_B - 1

    rsl = {a: slice(a * SUB_B, (a + 1) * SUB_B) for a in range(ns)}
    r_end, q_diag, k_diag, q_off, k_off = {}, {}, {}, {}, {}
    r_prev = {}
    zero_row = jnp.zeros((1, dk), F32)
    prev = zero_row
    for a in order:
        ca = cum[rsl[a]]
        mid = ca[mid_row:mid_row + 1]
        r_end[a] = ca[end_row:end_row + 1]
        r_prev[a] = prev
        q_diag[a] = (qs[rsl[a]] * jnp.exp2(ca - mid)).astype(BF16)
        k_diag[a] = (kf[rsl[a]] * jnp.exp2(mid - ca)).astype(BF16)
        q_off[a] = qs[rsl[a]] * jnp.exp2(ca - prev)
        k_off[a] = kf[rsl[a]] * jnp.exp2(r_end[a] - ca)
        prev = r_end[a]
    r_last = prev

    ri = lax.broadcasted_iota(jnp.int32, (SUB_B, SUB_B), 0)
    ci = lax.broadcasted_iota(jnp.int32, (SUB_B, SUB_B), 1)
    keep = (ci > ri) if reverse else (ci <= ri)

    k_off_b = {a: k_off[a].astype(BF16) for a in range(ns)}
    a_rows = []
    q_state = []
    for a in range(ns):
        blocks = []
        for b in range(ns):
            if b == a:
                blk = jnp.where(keep, _dot_nt(q_diag[a], k_diag[a]), 0.0)
            elif order.index(b) < order.index(a):
                if order.index(b) + 1 == order.index(a):
                    qa = q_off[a]
                else:
                    qa = q_off[a] * jnp.exp2(r_prev[a] - r_end[b])
                blk = _dot_nt(qa.astype(BF16), k_off_b[b])
            else:
                blk = jnp.zeros((SUB_B, SUB_B), F32)
            blocks.append(blk)
        a_rows.append(jnp.concatenate(blocks, axis=1) if ns > 1 else blocks[0])
        q_state.append(q_off[a] * jnp.exp2(r_prev[a]))
    a_full = jnp.concatenate(a_rows, axis=0) if ns > 1 else a_rows[0]
    q_st = jnp.concatenate(q_state, axis=0) if ns > 1 else q_state[0]

    a_b = a_full.astype(BF16)
    q_b = q_st.astype(BF16)
    k_dec = [k_off[a] * jnp.exp2(r_last - r_end[a]) for a in range(ns)]
    k_dec = (jnp.concatenate(k_dec, axis=0) if ns > 1 else k_dec[0]).astype(BF16)
    dcol = _row_to_col(jnp.exp2(r_last), 128)
    dcol = jnp.concatenate([dcol] * (v_blocks[0].shape[1] // 128), axis=1)
    o_blocks, new_state = [], []
    for vc, sc in zip(v_blocks, state_blocks):
        o_blocks.append(_dot(a_b, vc) + _dot(q_b, sc.astype(BF16)))
        new_state.append(sc * dcol + _dot_tn(k_dec, vc))
    return o_blocks, new_state


def _gla_sweep_tile(q_ref, k_ref, v_ref, cum_ref, s_ref, h, emit, reverse):
    tb = q_ref.shape[0]
    nvb = v_ref.shape[0]
    nm = tb // MACRO_B
    chunks = range(nm - 1, -1, -1) if reverse else range(nm)
    for m in chunks:
        rows = slice(m * MACRO_B, (m + 1) * MACRO_B)
        o_blocks, s_new = _gla_macro_chunk(
            q_ref[rows, :], k_ref[rows, :], [v_ref[c, rows, :] for c in range(nvb)], cum_ref[rows, :],
            [s_ref[h, c] for c in range(nvb)], reverse)
        for c in range(nvb):
            s_ref[h, c] = s_new[c]
        emit(rows, o_blocks)


def _gla_fwd_kernel(q_ref, k_ref, v_ref, cum_ref, o_ref, s_ref):
    i = pl.program_id(1)
    h = pl.program_id(2)

    @pl.when(i == 0)
    def _reset():
        s_ref[h] = jnp.zeros(s_ref.shape[1:], F32)

    def emit(rows, o_blocks):
        for c, o in enumerate(o_blocks):
            o_ref[c, rows, :] = o.astype(o_ref.dtype)

    _gla_sweep_tile(q_ref, k_ref, v_ref, cum_ref, s_ref, h, emit, reverse=False)


def _gla_dims(hmain, batch, key_dim, d_inner):
    rows = hmain.shape[1]
    dk = key_dim // GLA_HEADS
    dv = d_inner // GLA_HEADS
    assert dk == COL_B and dv % COL_B == 0
    tb = min(TB_G, rows // batch)
    nt = rows // batch // tb
    nvb = dv // COL_B
    kk = key_dim // COL_B
    kv = 2 * key_dim // (nvb * COL_B)
    kz = (2 * key_dim + d_inner) // (nvb * COL_B)
    return rows, dk, dv, tb, nt, nvb, kk, kv, kz


def _gla_fwd(hmain, cum_f, batch, key_dim, d_inner):
    rows, dk, dv, tb, nt, nvb, kk, kv, _ = _gla_dims(hmain, batch, key_dim, d_inner)
    grid = (batch, nt, GLA_HEADS)
    return pl.pallas_call(
        _gla_fwd_kernel,
        grid=grid,
        in_specs=[
            pl.BlockSpec((None, tb, dk), lambda b, i, h: (h, b * nt + i, 0)),
            pl.BlockSpec((None, tb, dk), lambda b, i, h: (kk + h, b * nt + i, 0)),
            pl.BlockSpec((nvb, tb, COL_B), lambda b, i, h: (kv + h, b * nt + i, 0)),
            pl.BlockSpec((None, tb, dk), lambda b, i, h: (h, b * nt + i, 0)),
        ],
        out_specs=pl.BlockSpec((nvb, tb, COL_B), lambda b, i, h: (h, b * nt + i, 0)),
        out_shape=jax.ShapeDtypeStruct((GLA_HEADS * nvb, rows, COL_B), BF16),
        scratch_shapes=[pltpu.VMEM((GLA_HEADS, nvb, dk, COL_B), F32)],
        compiler_params=pltpu.CompilerParams(
            dimension_semantics=("arbitrary", "arbitrary", "arbitrary"),
            vmem_limit_bytes=VMEM_LIMIT),
        name="gla_fwd",
    )(hmain, hmain, hmain, cum_f)


def _gla_bwd_kernel(q_ref, k_ref, v_ref, cum_ref, of_ref, z_ref, x_ref, wo_ref, gn_ref,
                    lng_ref, lnb_ref, o_ref, s_ref, y_ref, *, alpha):
    i = pl.program_id(1)
    h = pl.program_id(2)
    tb = o_ref.shape[0]
    xr = x_ref.shape[0]

    @pl.when(i == 0)
    def _reset():
        s_ref[h] = jnp.zeros(s_ref.shape[1:], F32)

    @pl.when(h == 0)
    def _zero_acc():
        o_ref[...] = jnp.zeros_like(o_ref)

    def emit(rows, o_blocks):
        nvb = len(o_blocks)
        o = [of_ref[c, rows, :].astype(F32) + o_blocks[c] for c in range(nvb)]
        ss = sum(jnp.sum(oc * oc, axis=-1, keepdims=True) for oc in o)
        rstd = lax.rsqrt(ss * (1.0 / (nvb * COL_B)) + RMS_EPS)
        for c in range(nvb):
            cols = slice(c * COL_B, (c + 1) * COL_B)
            y_ref[rows, cols] = (o[c] * rstd * gn_ref[:, cols]
                                 * _silu(z_ref[c, rows, :].astype(F32))).astype(BF16)

    _gla_sweep_tile(q_ref, k_ref, v_ref, cum_ref, s_ref, h, emit, reverse=True)

    o_ref[...] += _dot(y_ref[...], wo_ref[...])
    xrows = pl.ds(pl.multiple_of(h * xr, xr), xr)
    o_ref[xrows, :] += alpha * x_ref[...]

    @pl.when(h == GLA_HEADS - 1)
    def _finish():
        for r in range(tb // CUM_ROWS):
            rows = slice(r * CUM_ROWS, (r + 1) * CUM_ROWS)
            o_ref[rows, :] = _layer_norm_rows(o_ref[rows, :], lng_ref[...], lnb_ref[...])


def _gla_bwd(hmain, cum_b, o_f, x2, p, lng, lnb, batch, alpha):
    d = x2.shape[1]
    w_out = p["w_out"]
    key_dim = p["key_dim"]
    d_inner = w_out.shape[0]
    rows, dk, dv, tb, nt, nvb, kk, kv, kz = _gla_dims(hmain, batch, key_dim, d_inner)
    grid = (batch, nt, GLA_HEADS)

    def row(b, i):
        return b * nt + (nt - 1 - i)

    xr = tb // GLA_HEADS
    kernel = functools.partial(_gla_bwd_kernel, alpha=alpha)
    return pl.pallas_call(
        kernel,
        grid=grid,
        in_specs=[
            pl.BlockSpec((None, tb, dk), lambda b, i, h: (h, row(b, i), 0)),
            pl.BlockSpec((None, tb, dk), lambda b, i, h: (kk + h, row(b, i), 0)),
            pl.BlockSpec((nvb, tb, COL_B), lambda b, i, h: (kv + h, row(b, i), 0)),
            pl.BlockSpec((None, tb, dk), lambda b, i, h: (h, row(b, i), 0)),
            pl.BlockSpec((nvb, tb, COL_B), lambda b, i, h: (h, row(b, i), 0)),
            pl.BlockSpec((nvb, tb, COL_B), lambda b, i, h: (kz + h, row(b, i), 0)),
            pl.BlockSpec((xr, d), lambda b, i, h: (row(b, i) * GLA_HEADS + h, 0)),
            pl.BlockSpec((dv, d), lambda b, i, h: (h, 0)),
            pl.BlockSpec((1, dv), lambda b, i, h: (0, h)),
            pl.BlockSpec((1, d), lambda b, i, h: (0, 0)),
            pl.BlockSpec((1, d), lambda b, i, h: (0, 0)),
        ],
        out_specs=pl.BlockSpec((tb, d), lambda b, i, h: (row(b, i), 0)),
        out_shape=jax.ShapeDtypeStruct((rows, d), F32),
        scratch_shapes=[
            pltpu.VMEM((GLA_HEADS, nvb, dk, COL_B), F32),
            pltpu.VMEM((tb, dv), BF16),
        ],
        compiler_params=pltpu.CompilerParams(
            dimension_semantics=("arbitrary", "arbitrary", "arbitrary"),
            vmem_limit_bytes=VMEM_LIMIT),
        name="gla_bwd",
    )(hmain, hmain, hmain, cum_b, o_f, hmain, x2, w_out, p["gn"], lng, lnb)


def _tri_blocks(reverse):
    r = jnp.arange(CUM_ROWS)[:, None]
    c = jnp.arange(CUM_ROWS)[None, :]
    same = (r // MACRO_B) == (c // MACRO_B)
    tri = (c >= r) if reverse else (c <= r)
    return (same & tri).astype(BF16)


def _cast_block_kernel(w_ref, o_ref):
    o_ref[...] = w_ref[...].astype(BF16)


def _relayout_cols(w3, layer, cw, n_slots, src_chunk):
    k = w3.shape[1]
    return pl.pallas_call(
        _cast_block_kernel,
        grid=(n_slots,),
        in_specs=[pl.BlockSpec((None, k, cw), lambda s: (layer, 0, src_chunk(s)))],
        out_specs=pl.BlockSpec((None, k, cw), lambda s: (s, 0, 0)),
        out_shape=jax.ShapeDtypeStruct((n_slots, k, cw), BF16),
        compiler_params=pltpu.CompilerParams(dimension_semantics=("arbitrary",)),
        name="weight_relayout",
    )(w3)


def _prep_a(w_in3, layer, vg, vb, w_s, b_s, w_out):
    d = w_in3.shape[1]
    d_inner = w_out.shape[0]
    nc = d_inner // CW_A
    npv = nc // 2

    def src_chunk(s):
        pair, t = s // 2, s % 2
        return jnp.where(pair < npv, nc + s, jnp.where(t == 0, pair - npv, 2 * nc + pair - npv))

    w1 = _relayout_cols(w_in3, layer, CW_A, 3 * nc, src_chunk).reshape(npv + nc, 2, d, CW_A)
    return dict(w1=w1, w_s=w_s.astype(BF16),
                b_s=b_s[:, :, None].astype(F32), w_out=w_out.astype(BF16),
                vg=vg[None, :], vb=vb[None, :])


def _prep_b(w_in3, layer, w_g2, b_g, gn_g, w_out):
    key_dim = w_g2.shape[2]
    d_inner = w_out.shape[0]
    n_main = 2 * key_dim + 2 * d_inner
    zeros = jnp.zeros((GATE_RANK, key_dim), F32)
    w_gate = jnp.concatenate([jnp.concatenate([w_g2[0], zeros], axis=1),
                              jnp.concatenate([zeros, w_g2[1]], axis=1)], axis=0)
    return dict(w_main=_relayout_cols(w_in3[layer:layer + 1, :, :n_main], 0, CW_P, n_main // CW_P, lambda s: s),
                w_gl=w_in3[layer, :, n_main:].T.astype(BF16),
                w_gate=w_gate.astype(BF16), b_gate=b_g.reshape(1, 2 * key_dim),
                gn=gn_g[None, :], w_out=w_out.astype(BF16), key_dim=key_dim, d_inner=d_inner,
                tri_f=_tri_blocks(False), tri_b=_tri_blocks(True))


def _mixer_b_layer(x2, batch, p, lng, lnb, alpha):
    hmain, cum_f, cum_b = _proj_b(x2, p)
    o_f = _gla_fwd(hmain, cum_f, batch, p["key_dim"], p["d_inner"])
    return _gla_bwd(hmain, cum_b, o_f, x2, p, lng, lnb, batch, alpha)


def _trunk(x, layers, ln_g, ln_b, alpha):
    batch, seq, d = x.shape
    x2 = x.reshape(batch * seq, d)
    for i, (kind, p) in enumerate(layers):
        lng, lnb = ln_g[i][None, :], ln_b[i][None, :]
        if kind == "a":
            x2 = _layer_a(x2, p, lng, lnb, alpha)
        else:
            x2 = _mixer_b_layer(x2, batch, p, lng, lnb, alpha)
    return x2.reshape(batch, seq, d)


def kernel(x_prompt, x_sample, w_in_a, ln_v_g_a, ln_v_b_a, w_s_a, b_s_a, w_out_a, w_in_b, w_g2_b, b_g_b, gn_g_b, w_out_b, ln_g, ln_b):
    depth = ln_g.shape[0]
    alpha = (2 * depth) ** 0.25
    layers = []
    for i in range(depth):
        j = i // 2
        if i % 2 == 0:
            layers.append(("a", _prep_a(w_in_a, j, ln_v_g_a[j], ln_v_b_a[j], w_s_a[j], b_s_a[j], w_out_a[j])))
        else:
            layers.append(("b", _prep_b(w_in_b, j, w_g2_b[j], b_g_b[j], gn_g_b[j], w_out_b[j])))
    y_prompt = _trunk(x_prompt, layers, ln_g, ln_b, alpha)
    y_sample = _trunk(x_sample, layers, ln_g, ln_b, alpha)
    return (y_prompt, y_sample)
```

```python
import functools

import jax
import jax.numpy as jnp
from jax import lax
from jax.experimental import pallas as pl
from jax.experimental.pallas import tpu as pltpu

F32 = jnp.float32
BF16 = jnp.bfloat16

CHUNK_A = 128
N_GROUPS_A = 16
GLA_HEADS = 4
GATE_RANK = 16
GATE_TAU = 16.0
LOG2_E = 1.4426950408889634
SUB_B = 64
LN_EPS = 1e-5
RMS_EPS = 1e-6

TM_A = 1024
CW_A = 512
RB_A = 256
MXU_COLS = 256
TM_P = 1024
CW_P = 1024
TB_GF = 2048
TB_GB = 1024
MACRO_B = 256
CUM_ROWS = 256
COL_B = 256
VMEM_LIMIT = 60 * 1024 * 1024


def _dot(a, b):
    return jnp.dot(a, b, preferred_element_type=F32)


def _dot_nt(a, b):
    return lax.dot_general(a, b, (((1,), (1,)), ((), ())), preferred_element_type=F32)


def _dot_tn(a, b):
    return lax.dot_general(a, b, (((0,), (0,)), ((), ())), preferred_element_type=F32)


def _gelu_tanh(x):
    c = 0.7978845608028654
    hx = 0.5 * x
    return hx + hx * jnp.tanh(x * (c + (c * 0.044715) * (x * x)))


def _silu(x):
    hx = 0.5 * x
    return hx + hx * jnp.tanh(hx)


def _log_sigmoid(x):
    return jnp.minimum(x, 0.0) - jnp.log(1.0 + jnp.exp(-jnp.abs(x)))


def _layer_norm_rows(h, g, b):
    mu = jnp.mean(h, axis=-1, keepdims=True)
    hc = h - mu
    var = jnp.mean(hc * hc, axis=-1, keepdims=True)
    return hc * lax.rsqrt(var + LN_EPS) * g + b


def _layer_a_kernel(x_hbm, w1_ref, wo_ref, ws_ref, bs_ref, vg_ref, vb_ref,
                    lng_ref, lnb_ref, o_ref,
                    stage_ref, xb_ref, gv_ref, raw_ref, sum_ref, sq_ref, mu_ref, rstd_ref, y_ref, sem,
                    *, nc, d_inner, alpha, n_tiles):
    i = pl.program_id(0)
    j = pl.program_id(1)
    tm = o_ref.shape[0]
    cw = w1_ref.shape[2]
    gdim = d_inner // N_GROUPS_A
    npv = nc // 2
    lanes = sum_ref.shape[1]

    def x_copy(tile):
        return pltpu.make_async_copy(x_hbm.at[pl.ds(tile * tm, tm), :], stage_ref, sem)

    @pl.when((j == 0) & (i == 0))
    def _first_fetch():
        x_copy(0).start()

    def gelu_and_stats(chunk, v):
        for pc in range(cw // MXU_COLS):
            cols = slice(pc * MXU_COLS, (pc + 1) * MXU_COLS)
            gv = _gelu_tanh(v[:, cols])
            gv_ref[chunk, :, cols] = gv.astype(BF16)
            part = gv[:, 0:lanes]
            part2 = part * part
            for l in range(1, MXU_COLS // lanes):
                blk = gv[:, l * lanes:(l + 1) * lanes]
                part = part + blk
                part2 = part2 + blk * blk
            sum_ref[...] += part
            sq_ref[...] += part2

    def phase_v_body(first):
        if not first:
            gelu_and_stats(2 * j - 1, raw_ref[...])
        gelu_and_stats(2 * j, _dot(xb_ref[...], w1_ref[0]))
        raw_ref[...] = _dot(xb_ref[...], w1_ref[1])

    @pl.when(j == 0)
    def _first_step():
        x_copy(i).wait()
        for r in range(tm // RB_A):
            rows = slice(r * RB_A, (r + 1) * RB_A)
            x = stage_ref[rows, :]
            xb_ref[rows, :] = x.astype(BF16)
            o_ref[rows, :] = alpha * x
        sum_ref[...] = jnp.zeros_like(sum_ref)
        sq_ref[...] = jnp.zeros_like(sq_ref)
        phase_v_body(first=True)

    @pl.when((j == 1) & (i + 1 < n_tiles))
    def _prefetch():
        x_copy(i + 1).start()

    @pl.when((j > 0) & (j < npv))
    def _phase_v():
        phase_v_body(first=False)

    def finish_stats():
        gelu_and_stats(nc - 1, raw_ref[...])
        mu = jnp.sum(sum_ref[...], axis=1, keepdims=True) * (1.0 / d_inner)
        var = jnp.sum(sq_ref[...], axis=1, keepdims=True) * (1.0 / d_inner) - mu * mu
        mu_ref[...] = mu
        rstd_ref[...] = lax.rsqrt(var + LN_EPS)

    def mix_body(first):
        c = j - npv
        if first:
            finish_stats()
        xb = xb_ref[...]
        gate = _gelu_tanh(_dot(xb, w1_ref[0])) * _silu(_dot(xb, w1_ref[1]))
        vn = ((gv_ref[c].astype(F32) - mu_ref[...]) * rstd_ref[...] * vg_ref[...]
              + vb_ref[...]).astype(BF16)
        for g in range(cw // gdim):
            cols = slice(g * gdim, (g + 1) * gdim)
            wsg = ws_ref[g]
            bsg = jnp.broadcast_to(bs_ref[g], (CHUNK_A, gdim))
            for n in range(tm // CHUNK_A):
                rows = slice(n * CHUNK_A, (n + 1) * CHUNK_A)
                s = _dot(wsg, vn[rows, cols]) + bsg
                y_ref[rows, cols] = (gate[rows, cols] * s).astype(BF16)
        o_ref[...] += _dot(y_ref[...], wo_ref[...])

    @pl.when(j == npv)
    def _first_mix():
        mix_body(first=True)

    @pl.when(j > npv)
    def _mix():
        mix_body(first=False)

    @pl.when(j == npv + nc - 1)
    def _finish():
        for r in range(tm // RB_A):
            rows = slice(r * RB_A, (r + 1) * RB_A)
            o_ref[rows, :] = _layer_norm_rows(o_ref[rows, :], lng_ref[...], lnb_ref[...])


def _layer_a(x2, p, lng, lnb, alpha):
    rows, d = x2.shape
    w1, w_out = p["w1"], p["w_out"]
    d_inner = w_out.shape[0]
    cw = w1.shape[3]
    nc = d_inner // cw
    npv = nc // 2
    gpc = cw // (d_inner // N_GROUPS_A)
    tm = min(TM_A, rows)
    grid = (rows // tm, npv + nc)

    def cmix(j):
        return jnp.maximum(j - npv, 0)

    kernel = functools.partial(_layer_a_kernel, nc=nc, d_inner=d_inner, alpha=alpha, n_tiles=grid[0])
    return pl.pallas_call(
        kernel,
        grid=grid,
        in_specs=[
            pl.BlockSpec(memory_space=pl.ANY),
            pl.BlockSpec((None, 2, d, cw), lambda i, j: (j, 0, 0, 0)),
            pl.BlockSpec((cw, d), lambda i, j: (cmix(j), 0)),
            pl.BlockSpec((gpc, CHUNK_A, CHUNK_A), lambda i, j: (cmix(j), 0, 0)),
            pl.BlockSpec((gpc, CHUNK_A, 1), lambda i, j: (cmix(j), 0, 0)),
            pl.BlockSpec((1, cw), lambda i, j: (0, cmix(j))),
            pl.BlockSpec((1, cw), lambda i, j: (0, cmix(j))),
            pl.BlockSpec((1, d), lambda i, j: (0, 0)),
            pl.BlockSpec((1, d), lambda i, j: (0, 0)),
        ],
        out_specs=pl.BlockSpec((tm, d), lambda i, j: (i, 0)),
        out_shape=jax.ShapeDtypeStruct((rows, d), F32),
        scratch_shapes=[
            pltpu.VMEM((tm, d), F32),
            pltpu.VMEM((tm, d), BF16),
            pltpu.VMEM((nc, tm, cw), BF16),
            pltpu.VMEM((tm, cw), F32),
            pltpu.VMEM((tm, 128), F32),
            pltpu.VMEM((tm, 128), F32),
            pltpu.VMEM((tm, 1), F32),
            pltpu.VMEM((tm, 1), F32),
            pltpu.VMEM((tm, cw), BF16),
            pltpu.SemaphoreType.DMA,
        ],
        compiler_params=pltpu.CompilerParams(
            dimension_semantics=("arbitrary", "arbitrary"),
            vmem_limit_bytes=VMEM_LIMIT),
        name="layer_a",
    )(x2, w1, w_out, p["w_s"], p["b_s"], p["vg"], p["vb"], lng, lnb)


def _proj_b_kernel(x_ref, w_ref, wgl_ref, wg_ref, bg_ref, lf_ref, lb_ref,
                   h_ref, gf_ref, rb_ref, xb_ref, gl_ref, *, key_dim):
    j = pl.program_id(1)
    tm = x_ref.shape[0]

    def project():
        res = _dot(xb_ref[...], w_ref[...])
        for c in range(res.shape[1] // COL_B):
            h_ref[c] = res[:, c * COL_B:(c + 1) * COL_B].astype(BF16)

    def gates(r):
        rows = slice(r * CUM_ROWS, (r + 1) * CUM_ROWS)
        pre = _dot(gl_ref[rows, :], wg_ref[...]) + bg_ref[...]
        glog = _log_sigmoid(pre) * (LOG2_E / GATE_TAU)
        for dst, tri, cols in ((gf_ref, lf_ref, slice(0, key_dim)),
                               (rb_ref, lb_ref, slice(key_dim, 2 * key_dim))):
            gb = glog[:, cols]
            hi = gb.astype(BF16)
            lo = (gb - hi.astype(F32)).astype(BF16)
            cs = _dot(tri[...], hi) + _dot(tri[...], lo)
            for c in range(key_dim // COL_B):
                dst[c, rows, :] = cs[:, c * COL_B:(c + 1) * COL_B]

    @pl.when(j == 0)
    def _gates_and_first_chunk():
        for r in range(tm // CUM_ROWS):
            rows = slice(r * CUM_ROWS, (r + 1) * CUM_ROWS)
            xb_ref[rows, :] = x_ref[rows, :].astype(BF16)
        gl_t = _dot_nt(wgl_ref[...], xb_ref[...])
        gl_ref[...] = gl_t.T.astype(BF16)
        for r in range(tm // CUM_ROWS):
            gates(r)
        project()

    @pl.when(j > 0)
    def _other_chunks():
        project()


def _proj_b(x2, p):
    rows, d = x2.shape
    w_main = p["w_main"]
    n_chunks, _, cw = w_main.shape
    key_dim = p["key_dim"]
    tm = min(TM_P, rows)
    grid = (rows // tm, n_chunks)
    kernel = functools.partial(_proj_b_kernel, key_dim=key_dim)
    small = [p["w_gl"], p["w_gate"], p["b_gate"], p["tri_f"], p["tri_b"]]
    return pl.pallas_call(
        kernel,
        grid=grid,
        in_specs=[
            pl.BlockSpec((tm, d), lambda i, j: (i, 0)),
            pl.BlockSpec((None, d, cw), lambda i, j: (j, 0, 0)),
        ] + [pl.BlockSpec(a.shape, lambda i, j: (0, 0)) for a in small],
        out_specs=[
            pl.BlockSpec((cw // COL_B, tm, COL_B), lambda i, j: (j, i, 0)),
            pl.BlockSpec((key_dim // COL_B, tm, COL_B), lambda i, j: (0, i, 0)),
            pl.BlockSpec((key_dim // COL_B, tm, COL_B), lambda i, j: (0, i, 0)),
        ],
        out_shape=[
            jax.ShapeDtypeStruct((n_chunks * cw // COL_B, rows, COL_B), BF16),
            jax.ShapeDtypeStruct((key_dim // COL_B, rows, COL_B), F32),
            jax.ShapeDtypeStruct((key_dim // COL_B, rows, COL_B), F32),
        ],
        scratch_shapes=[pltpu.VMEM((tm, d), BF16),
                        pltpu.VMEM((tm, p["w_gl"].shape[0]), BF16)],
        compiler_params=pltpu.CompilerParams(
            dimension_semantics=("arbitrary", "arbitrary"),
            vmem_limit_bytes=VMEM_LIMIT),
        name="proj_b",
    )(x2, w_main, *small)


def _row_to_col(vec_row, width):
    n = vec_row.shape[1]
    return jnp.broadcast_to(vec_row, (width, n)).T


def _gla_macro_chunk(q, k, v_blocks, cum, state_blocks, reverse):
    c_rows, dk = q.shape
    ns = c_rows // SUB_B
    qs = q.astype(F32) * (dk ** -0.5)
    kf = k.astype(F32)
    order = list(range(ns))[::-1] if reverse else list(range(ns))
    mid_row = SUB_B // 2 - 1 if reverse else SUB_B // 2
    end_row = 0 if reverse else SUB_B - 1

    rsl = {a: slice(a * SUB_B, (a + 1) * SUB_B) for a in range(ns)}
    r_end, q_diag, k_diag, q_off, k_off = {}, {}, {}, {}, {}
    r_prev = {}
    zero_row = jnp.zeros((1, dk), F32)
    prev = zero_row
    for a in order:
        ca = cum[rsl[a]]
        mid = ca[mid_row:mid_row + 1]
        r_end[a] = ca[end_row:end_row + 1]
        r_prev[a] = prev
        q_diag[a] = (qs[rsl[a]] * jnp.exp2(ca - mid)).astype(BF16)
        k_diag[a] = (kf[rsl[a]] * jnp.exp2(mid - ca)).astype(BF16)
        q_off[a] = qs[rsl[a]] * jnp.exp2(ca - prev)
        k_off[a] = kf[rsl[a]] * jnp.exp2(r_end[a] - ca)
        prev = r_end[a]
    r_last = prev

    ri = lax.broadcasted_iota(jnp.int32, (SUB_B, SUB_B), 0)
    ci = lax.broadcasted_iota(jnp.int32, (SUB_B, SUB_B), 1)
    keep = (ci > ri) if reverse else (ci <= ri)

    k_off_b = {a: k_off[a].astype(BF16) for a in range(ns)}
    a_rows = []
    q_state = []
    for a in range(ns):
        blocks = []
        for b in range(ns):
            if b == a:
                blk = jnp.where(keep, _dot_nt(q_diag[a], k_diag[a]), 0.0)
            elif order.index(b) < order.index(a):
                if order.index(b) + 1 == order.index(a):
                    qa = q_off[a]
                else:
                    qa = q_off[a] * jnp.exp2(r_prev[a] - r_end[b])
                blk = _dot_nt(qa.astype(BF16), k_off_b[b])
            else:
                blk = jnp.zeros((SUB_B, SUB_B), F32)
            blocks.append(blk)
        a_rows.append(jnp.concatenate(blocks, axis=1) if ns > 1 else blocks[0])
        q_state.append(q_off[a] * jnp.exp2(r_prev[a]))
    a_full = jnp.concatenate(a_rows, axis=0) if ns > 1 else a_rows[0]
    q_st = jnp.concatenate(q_state, axis=0) if ns > 1 else q_state[0]

    a_b = a_full.astype(BF16)
    q_b = q_st.astype(BF16)
    k_dec = [k_off[a] * jnp.exp2(r_last - r_end[a]) for a in range(ns)]
    k_dec = (jnp.concatenate(k_dec, axis=0) if ns > 1 else k_dec[0]).astype(BF16)
    dcol = _row_to_col(jnp.exp2(r_last), 128)
    dcol = jnp.concatenate([dcol] * (v_blocks[0].shape[1] // 128), axis=1)
    o_blocks, new_state = [], []
    for vc, sc in zip(v_blocks, state_blocks):
        o_blocks.append(_dot(a_b, vc) + _dot(q_b, sc.astype(BF16)))
        new_state.append(sc * dcol + _dot_tn(k_dec, vc))
    return o_blocks, new_state


def _gla_sweep_tile(q_ref, k_ref, v_ref, cum_ref, s_ref, h, emit, reverse):
    tb = q_ref.shape[0]
    nvb = v_ref.shape[0]
    nm = tb // MACRO_B
    chunks = range(nm - 1, -1, -1) if reverse else range(nm)
    for m in chunks:
        rows = slice(m * MACRO_B, (m + 1) * MACRO_B)
        o_blocks, s_new = _gla_macro_chunk(
            q_ref[rows, :], k_ref[rows, :], [v_ref[c, rows, :] for c in range(nvb)], cum_ref[rows, :],
            [s_ref[h, c] for c in range(nvb)], reverse)
        for c in range(nvb):
            s_ref[h, c] = s_new[c]
        emit(rows, o_blocks)


def _gla_fwd_kernel(q_ref, k_ref, v_ref, cum_ref, o_ref, s_ref):
    i = pl.program_id(1)
    h = pl.program_id(2)

    @pl.when(i == 0)
    def _reset():
        s_ref[h] = jnp.zeros(s_ref.shape[1:], F32)

    def emit(rows, o_blocks):
        for c, o in enumerate(o_blocks):
            o_ref[c, rows, :] = o.astype(o_ref.dtype)

    _gla_sweep_tile(q_ref, k_ref, v_ref, cum_ref, s_ref, h, emit, reverse=False)


def _gla_dims(hmain, batch, key_dim, d_inner, tile):
    rows = hmain.shape[1]
    dk = key_dim // GLA_HEADS
    dv = d_inner // GLA_HEADS
    assert dk == COL_B and dv % COL_B == 0
    tb = min(tile, rows // batch)
    nt = rows // batch // tb
    nvb = dv // COL_B
    kk = key_dim // COL_B
    kv = 2 * key_dim // (nvb * COL_B)
    kz = (2 * key_dim + d_inner) // (nvb * COL_B)
    return rows, dk, dv, tb, nt, nvb, kk, kv, kz


def _gla_fwd(hmain, cum_f, batch, key_dim, d_inner):
    rows, dk, dv, tb, nt, nvb, kk, kv, _ = _gla_dims(hmain, batch, key_dim, d_inner, TB_GF)
    grid = (batch, nt, GLA_HEADS)
    return pl.pallas_call(
        _gla_fwd_kernel,
        grid=grid,
        in_specs=[
            pl.BlockSpec((None, tb, dk), lambda b, i, h: (h, b * nt + i, 0)),
            pl.BlockSpec((None, tb, dk), lambda b, i, h: (kk + h, b * nt + i, 0)),
            pl.BlockSpec((nvb, tb, COL_B), lambda b, i, h: (kv + h, b * nt + i, 0)),
            pl.BlockSpec((None, tb, dk), lambda b, i, h: (h, b * nt + i, 0)),
        ],
        out_specs=pl.BlockSpec((nvb, tb, COL_B), lambda b, i, h: (h, b * nt + i, 0)),
        out_shape=jax.ShapeDtypeStruct((GLA_HEADS * nvb, rows, COL_B), BF16),
        scratch_shapes=[pltpu.VMEM((GLA_HEADS, nvb, dk, COL_B), F32)],
        compiler_params=pltpu.CompilerParams(
            dimension_semantics=("arbitrary", "arbitrary", "arbitrary"),
            vmem_limit_bytes=VMEM_LIMIT),
        name="gla_fwd",
    )(hmain, hmain, hmain, cum_f)


def _gla_bwd_kernel(q_ref, k_ref, v_ref, cum_ref, of_ref, z_ref, x_ref, wo_ref, gn_ref,
                    lng_ref, lnb_ref, o_ref, s_ref, y_ref, *, alpha):
    i = pl.program_id(1)
    h = pl.program_id(2)
    tb = o_ref.shape[0]
    xr = x_ref.shape[0]

    @pl.when(i == 0)
    def _reset():
        s_ref[h] = jnp.zeros(s_ref.shape[1:], F32)

    @pl.when(h == 0)
    def _zero_acc():
        o_ref[...] = jnp.zeros_like(o_ref)

    def emit(rows, o_blocks):
        nvb = len(o_blocks)
        o = [of_ref[c, rows, :].astype(F32) + o_blocks[c] for c in range(nvb)]
        ss = sum(jnp.sum(oc * oc, axis=-1, keepdims=True) for oc in o)
        rstd = lax.rsqrt(ss * (1.0 / (nvb * COL_B)) + RMS_EPS)
        for c in range(nvb):
            cols = slice(c * COL_B, (c + 1) * COL_B)
            y_ref[rows, cols] = (o[c] * rstd * gn_ref[:, cols]
                                 * _silu(z_ref[c, rows, :].astype(F32))).astype(BF16)

    _gla_sweep_tile(q_ref, k_ref, v_ref, cum_ref, s_ref, h, emit, reverse=True)

    o_ref[...] += _dot(y_ref[...], wo_ref[...])
    xrows = pl.ds(pl.multiple_of(h * xr, xr), xr)
    o_ref[xrows, :] += alpha * x_ref[...]

    @pl.when(h == GLA_HEADS - 1)
    def _finish():
        for r in range(tb // CUM_ROWS):
            rows = slice(r * CUM_ROWS, (r + 1) * CUM_ROWS)
            o_ref[rows, :] = _layer_norm_rows(o_ref[rows, :], lng_ref[...], lnb_ref[...])


def _gla_bwd(hmain, cum_b, o_f, x2, p, lng, lnb, batch, alpha):
    d = x2.shape[1]
    w_out = p["w_out"]
    key_dim = p["key_dim"]
    d_inner = w_out.shape[0]
    rows, dk, dv, tb, nt, nvb, kk, kv, kz = _gla_dims(hmain, batch, key_dim, d_inner, TB_GB)
    grid = (batch, nt, GLA_HEADS)

    def row(b, i):
        return b * nt + (nt - 1 - i)

    xr = tb // GLA_HEADS
    kernel = functools.partial(_gla_bwd_kernel, alpha=alpha)
    return pl.pallas_call(
        kernel,
        grid=grid,
        in_specs=[
            pl.BlockSpec((None, tb, dk), lambda b, i, h: (h, row(b, i), 0)),
            pl.BlockSpec((None, tb, dk), lambda b, i, h: (kk + h, row(b, i), 0)),
            pl.BlockSpec((nvb, tb, COL_B), lambda b, i, h: (kv + h, row(b, i), 0)),
            pl.BlockSpec((None, tb, dk), lambda b, i, h: (h, row(b, i), 0)),
            pl.BlockSpec((nvb, tb, COL_B), lambda b, i, h: (h, row(b, i), 0)),
            pl.BlockSpec((nvb, tb, COL_B), lambda b, i, h: (kz + h, row(b, i), 0)),
            pl.BlockSpec((xr, d), lambda b, i, h: (row(b, i) * GLA_HEADS + h, 0)),
            pl.BlockSpec((dv, d), lambda b, i, h: (h, 0)),
            pl.BlockSpec((1, dv), lambda b, i, h: (0, h)),
            pl.BlockSpec((1, d), lambda b, i, h: (0, 0)),
            pl.BlockSpec((1, d), lambda b, i, h: (0, 0)),
        ],
        out_specs=pl.BlockSpec((tb, d), lambda b, i, h: (row(b, i), 0)),
        out_shape=jax.ShapeDtypeStruct((rows, d), F32),
        scratch_shapes=[
            pltpu.VMEM((GLA_HEADS, nvb, dk, COL_B), F32),
            pltpu.VMEM((tb, dv), BF16),
        ],
        compiler_params=pltpu.CompilerParams(
            dimension_semantics=("arbitrary", "arbitrary", "arbitrary"),
            vmem_limit_bytes=VMEM_LIMIT),
        name="gla_bwd",
    )(hmain, hmain, hmain, cum_b, o_f, hmain, x2, w_out, p["gn"], lng, lnb)


def _tri_blocks(reverse):
    r = jnp.arange(CUM_ROWS)[:, None]
    c = jnp.arange(CUM_ROWS)[None, :]
    same = (r // MACRO_B) == (c // MACRO_B)
    tri = (c >= r) if reverse else (c <= r)
    return (same & tri).astype(BF16)


def _cast_block_kernel(w_ref, o_ref):
    o_ref[...] = w_ref[...].astype(BF16)


def _relayout_cols(w, cw, n_slots, src_chunk):
    k = w.shape[0]
    return pl.pallas_call(
        _cast_block_kernel,
        grid=(n_slots,),
        in_specs=[pl.BlockSpec((k, cw), lambda s: (0, src_chunk(s)))],
        out_specs=pl.BlockSpec((None, k, cw), lambda s: (s, 0, 0)),
        out_shape=jax.ShapeDtypeStruct((n_slots, k, cw), BF16),
        compiler_params=pltpu.CompilerParams(dimension_semantics=("arbitrary",)),
        name="weight_relayout",
    )(w)


def _prep_a(w_in3, layer, vg, vb, w_s, b_s, w_out):
    d = w_in3.shape[1]
    d_inner = w_out.shape[0]
    nc = d_inner // CW_A
    npv = nc // 2

    def src_chunk(s):
        pair, t = s // 2, s % 2
        return jnp.where(pair < npv, nc + s, jnp.where(t == 0, pair - npv, 2 * nc + pair - npv))

    w1 = _relayout_cols(w_in3[layer], CW_A, 3 * nc, src_chunk).reshape(npv + nc, 2, d, CW_A)
    return dict(w1=w1, w_s=w_s.astype(BF16),
                b_s=b_s[:, :, None].astype(F32), w_out=w_out.astype(BF16),
                vg=vg[None, :], vb=vb[None, :])


def _prep_b(w_in3, layer, w_g2, b_g, gn_g, w_out):
    key_dim = w_g2.shape[2]
    d_inner = w_out.shape[0]
    n_main = 2 * key_dim + 2 * d_inner
    zeros = jnp.zeros((GATE_RANK, key_dim), F32)
    w_gate = jnp.concatenate([jnp.concatenate([w_g2[0], zeros], axis=1),
                              jnp.concatenate([zeros, w_g2[1]], axis=1)], axis=0)
    return dict(w_main=_relayout_cols(w_in3[layer], CW_P, n_main // CW_P, lambda s: s),
                w_gl=w_in3[layer, :, n_main:].T.astype(BF16),
                w_gate=w_gate.astype(BF16), b_gate=b_g.reshape(1, 2 * key_dim),
                gn=gn_g[None, :], w_out=w_out.astype(BF16), key_dim=key_dim, d_inner=d_inner,
                tri_f=_tri_blocks(False), tri_b=_tri_blocks(True))


def _mixer_b_layer(x2, batch, p, lng, lnb, alpha):
    hmain, cum_f, cum_b = _proj_b(x2, p)
    o_f = _gla_fwd(hmain, cum_f, batch, p["key_dim"], p["d_inner"])
    return _gla_bwd(hmain, cum_b, o_f, x2, p, lng, lnb, batch, alpha)


def _trunk(x, layers, ln_g, ln_b, alpha):
    batch, seq, d = x.shape
    x2 = x.reshape(batch * seq, d)
    for i, (kind, p) in enumerate(layers):
        lng, lnb = ln_g[i][None, :], ln_b[i][None, :]
        if kind == "a":
            x2 = _layer_a(x2, p, lng, lnb, alpha)
        else:
            x2 = _mixer_b_layer(x2, batch, p, lng, lnb, alpha)
    return x2.reshape(batch, seq, d)


def kernel(x_prompt, x_sample, w_in_a, ln_v_g_a, ln_v_b_a, w_s_a, b_s_a, w_out_a, w_in_b, w_g2_b, b_g_b, gn_g_b, w_out_b, ln_g, ln_b):
    depth = ln_g.shape[0]
    alpha = (2 * depth) ** 0.25
    layers = []
    for i in range(depth):
        j = i // 2
        if i % 2 == 0:
            layers.append(("a", _prep_a(w_in_a, j, ln_v_g_a[j], ln_v_b_a[j], w_s_a[j], b_s_a[j], w_out_a[j])))
        else:
            layers.append(("b", _prep_b(w_in_b, j, w_g2_b[j], b_g_b[j], gn_g_b[j], w_out_b[j])))
    y_prompt = _trunk(x_prompt, layers, ln_g, ln_b, alpha)
    y_sample = _trunk(x_sample, layers, ln_g, ln_b, alpha)
    return (y_prompt, y_sample)
```

```python
import functools

import jax
import jax.numpy as jnp
from jax import lax
from jax.experimental import pallas as pl
from jax.experimental.pallas import tpu as pltpu

F32 = jnp.float32
BF16 = jnp.bfloat16

CHUNK_A = 128
N_GROUPS_A = 16
GLA_HEADS = 4
GATE_RANK = 16
GATE_TAU = 16.0
LOG2_E = 1.4426950408889634
SUB_B = 64
LN_EPS = 1e-5
RMS_EPS = 1e-6

TM_A = 1024
CW_A = 512
RB_A = 256
MXU_COLS = 256
TM_P = 1024
CW_P = 1024
TB_GF = 2048
TB_GB = 1024
MACRO_B = 256
CUM_ROWS = 256
COL_B = 256
VMEM_LIMIT = 60 * 1024 * 1024


def _dot(a, b):
    return jnp.dot(a, b, preferred_element_type=F32)


def _dot_nt(a, b):
    return lax.dot_general(a, b, (((1,), (1,)), ((), ())), preferred_element_type=F32)


def _dot_tn(a, b):
    return lax.dot_general(a, b, (((0,), (0,)), ((), ())), preferred_element_type=F32)


def _gelu_tanh(x):
    c = 0.7978845608028654
    hx = 0.5 * x
    return hx + hx * jnp.tanh(x * (c + (c * 0.044715) * (x * x)))


def _silu(x):
    hx = 0.5 * x
    return hx + hx * jnp.tanh(hx)


def _log_sigmoid(x):
    return jnp.minimum(x, 0.0) - jnp.log(1.0 + jnp.exp(-jnp.abs(x)))


def _layer_norm_rows(h, g, b):
    mu = jnp.mean(h, axis=-1, keepdims=True)
    hc = h - mu
    var = jnp.mean(hc * hc, axis=-1, keepdims=True)
    return hc * lax.rsqrt(var + LN_EPS) * g + b


def _layer_a_kernel(x_hbm, w1_ref, wo_ref, ws_ref, bs_ref, vg_ref, vb_ref,
                    lng_ref, lnb_ref, o_ref,
                    stage_ref, xb_ref, gv_ref, raw_ref, sum_ref, sq_ref, mu_ref, rstd_ref, y_ref, sem,
                    *, nc, d_inner, alpha, n_tiles):
    i = pl.program_id(0)
    j = pl.program_id(1)
    tm = o_ref.shape[0]
    cw = w1_ref.shape[2]
    gdim = d_inner // N_GROUPS_A
    npv = nc // 2
    lanes = sum_ref.shape[1]

    def x_copy(tile):
        return pltpu.make_async_copy(x_hbm.at[pl.ds(tile * tm, tm), :], stage_ref, sem)

    @pl.when((j == 0) & (i == 0))
    def _first_fetch():
        x_copy(0).start()

    def gelu_and_stats(chunk, v):
        for pc in range(cw // MXU_COLS):
            cols = slice(pc * MXU_COLS, (pc + 1) * MXU_COLS)
            gv = _gelu_tanh(v[:, cols])
            gv_ref[chunk, :, cols] = gv.astype(BF16)
            part = gv[:, 0:lanes]
            part2 = part * part
            for l in range(1, MXU_COLS // lanes):
                blk = gv[:, l * lanes:(l + 1) * lanes]
                part = part + blk
                part2 = part2 + blk * blk
            sum_ref[...] += part
            sq_ref[...] += part2

    def phase_v_body(first):
        if not first:
            gelu_and_stats(2 * j - 1, raw_ref[...])
        gelu_and_stats(2 * j, _dot(xb_ref[...], w1_ref[0]))
        raw_ref[...] = _dot(xb_ref[...], w1_ref[1])

    @pl.when(j == 0)
    def _first_step():
        x_copy(i).wait()
        for r in range(tm // RB_A):
            rows = slice(r * RB_A, (r + 1) * RB_A)
            x = stage_ref[rows, :]
            xb_ref[rows, :] = x.astype(BF16)
            o_ref[rows, :] = alpha * x
        sum_ref[...] = jnp.zeros_like(sum_ref)
        sq_ref[...] = jnp.zeros_like(sq_ref)
        phase_v_body(first=True)

    @pl.when((j == 1) & (i + 1 < n_tiles))
    def _prefetch():
        x_copy(i + 1).start()

    @pl.when((j > 0) & (j < npv))
    def _phase_v():
        phase_v_body(first=False)

    def finish_stats():
        gelu_and_stats(nc - 1, raw_ref[...])
        mu = jnp.sum(sum_ref[...], axis=1, keepdims=True) * (1.0 / d_inner)
        var = jnp.sum(sq_ref[...], axis=1, keepdims=True) * (1.0 / d_inner) - mu * mu
        mu_ref[...] = mu
        rstd_ref[...] = lax.rsqrt(var + LN_EPS)

    def mix_body(first=False, last=False):
        c = j - npv
        if first:
            finish_stats()
        xb = xb_ref[...]
        gate = _gelu_tanh(_dot(xb, w1_ref[0])) * _silu(_dot(xb, w1_ref[1]))
        vn = ((gv_ref[c].astype(F32) - mu_ref[...]) * rstd_ref[...] * vg_ref[...]
              + vb_ref[...]).astype(BF16)
        for g in range(cw // gdim):
            cols = slice(g * gdim, (g + 1) * gdim)
            wsg = ws_ref[g]
            bsg = jnp.broadcast_to(bs_ref[g], (CHUNK_A, gdim))
            for n in range(tm // CHUNK_A):
                rows = slice(n * CHUNK_A, (n + 1) * CHUNK_A)
                s = _dot(wsg, vn[rows, cols]) + bsg
                y_ref[rows, cols] = (gate[rows, cols] * s).astype(BF16)
        if not last:
            o_ref[...] += _dot(y_ref[...], wo_ref[...])
        else:
            for r in range(tm // RB_A):
                rows = slice(r * RB_A, (r + 1) * RB_A)
                acc = o_ref[rows, :] + _dot(y_ref[rows, :], wo_ref[...])
                o_ref[rows, :] = _layer_norm_rows(acc, lng_ref[...], lnb_ref[...])

    @pl.when(j == npv)
    def _first_mix():
        mix_body(first=True)

    @pl.when((j > npv) & (j < npv + nc - 1))
    def _mix():
        mix_body()

    @pl.when(j == npv + nc - 1)
    def _last_mix():
        mix_body(last=True)


def _layer_a(x2, p, lng, lnb, alpha):
    rows, d = x2.shape
    w1, w_out = p["w1"], p["w_out"]
    d_inner = w_out.shape[0]
    cw = w1.shape[3]
    nc = d_inner // cw
    npv = nc // 2
    gpc = cw // (d_inner // N_GROUPS_A)
    tm = min(TM_A, rows)
    grid = (rows // tm, npv + nc)

    def cmix(j):
        return jnp.maximum(j - npv, 0)

    kernel = functools.partial(_layer_a_kernel, nc=nc, d_inner=d_inner, alpha=alpha, n_tiles=grid[0])
    return pl.pallas_call(
        kernel,
        grid=grid,
        in_specs=[
            pl.BlockSpec(memory_space=pl.ANY),
            pl.BlockSpec((None, 2, d, cw), lambda i, j: (j, 0, 0, 0)),
            pl.BlockSpec((cw, d), lambda i, j: (cmix(j), 0)),
            pl.BlockSpec((gpc, CHUNK_A, CHUNK_A), lambda i, j: (cmix(j), 0, 0)),
            pl.BlockSpec((gpc, CHUNK_A, 1), lambda i, j: (cmix(j), 0, 0)),
            pl.BlockSpec((1, cw), lambda i, j: (0, cmix(j))),
            pl.BlockSpec((1, cw), lambda i, j: (0, cmix(j))),
            pl.BlockSpec((1, d), lambda i, j: (0, 0)),
            pl.BlockSpec((1, d), lambda i, j: (0, 0)),
        ],
        out_specs=pl.BlockSpec((tm, d), lambda i, j: (i, 0)),
        out_shape=jax.ShapeDtypeStruct((rows, d), F32),
        scratch_shapes=[
            pltpu.VMEM((tm, d), F32),
            pltpu.VMEM((tm, d), BF16),
            pltpu.VMEM((nc, tm, cw), BF16),
            pltpu.VMEM((tm, cw), F32),
            pltpu.VMEM((tm, 128), F32),
            pltpu.VMEM((tm, 128), F32),
            pltpu.VMEM((tm, 1), F32),
            pltpu.VMEM((tm, 1), F32),
            pltpu.VMEM((tm, cw), BF16),
            pltpu.SemaphoreType.DMA,
        ],
        compiler_params=pltpu.CompilerParams(
            dimension_semantics=("arbitrary", "arbitrary"),
            vmem_limit_bytes=VMEM_LIMIT),
        name="layer_a",
    )(x2, w1, w_out, p["w_s"], p["b_s"], p["vg"], p["vb"], lng, lnb)


def _proj_b_kernel(x_ref, w_ref, wgl_ref, wg_ref, bg_ref, lf_ref, lb_ref,
                   h_ref, gf_ref, rb_ref, xb_ref, gl_ref, *, key_dim, z_chunk0):
    j = pl.program_id(1)
    tm = x_ref.shape[0]

    def project(gate_act=False):
        for c in range(h_ref.shape[0]):
            res = _dot(xb_ref[...], w_ref[:, c * COL_B:(c + 1) * COL_B])
            h_ref[c] = (_silu(res) if gate_act else res).astype(BF16)

    def gates(r):
        rows = slice(r * CUM_ROWS, (r + 1) * CUM_ROWS)
        pre = _dot(gl_ref[rows, :], wg_ref[...]) + bg_ref[...]
        glog = _log_sigmoid(pre) * (LOG2_E / GATE_TAU)
        for dst, tri, cols in ((gf_ref, lf_ref, slice(0, key_dim)),
                               (rb_ref, lb_ref, slice(key_dim, 2 * key_dim))):
            gb = glog[:, cols]
            hi = gb.astype(BF16)
            lo = (gb - hi.astype(F32)).astype(BF16)
            cs = _dot(tri[...], hi) + _dot(tri[...], lo)
            for c in range(key_dim // COL_B):
                dst[c, rows, :] = cs[:, c * COL_B:(c + 1) * COL_B]

    @pl.when(j == 0)
    def _gates_and_first_chunk():
        for r in range(tm // CUM_ROWS):
            rows = slice(r * CUM_ROWS, (r + 1) * CUM_ROWS)
            xb_ref[rows, :] = x_ref[rows, :].astype(BF16)
        gl_t = _dot_nt(wgl_ref[...], xb_ref[...])
        gl_ref[...] = gl_t.T.astype(BF16)
        for r in range(tm // CUM_ROWS):
            gates(r)
        project()

    @pl.when((j > 0) & (j < z_chunk0))
    def _qkv_chunks():
        project()

    @pl.when(j >= z_chunk0)
    def _z_chunks():
        project(gate_act=True)


def _proj_b(x2, p):
    rows, d = x2.shape
    w_main = p["w_main"]
    n_chunks, _, cw = w_main.shape
    key_dim = p["key_dim"]
    tm = min(TM_P, rows)
    grid = (rows // tm, n_chunks)
    z_chunk0 = (2 * key_dim + p["d_inner"]) // cw
    assert z_chunk0 * cw == 2 * key_dim + p["d_inner"] and z_chunk0 > 0
    kernel = functools.partial(_proj_b_kernel, key_dim=key_dim, z_chunk0=z_chunk0)
    small = [p["w_gl"], p["w_gate"], p["b_gate"], p["tri_f"], p["tri_b"]]
    return pl.pallas_call(
        kernel,
        grid=grid,
        in_specs=[
            pl.BlockSpec((tm, d), lambda i, j: (i, 0)),
            pl.BlockSpec((None, d, cw), lambda i, j: (j, 0, 0)),
        ] + [pl.BlockSpec(a.shape, lambda i, j: (0, 0)) for a in small],
        out_specs=[
            pl.BlockSpec((cw // COL_B, tm, COL_B), lambda i, j: (j, i, 0)),
            pl.BlockSpec((key_dim // COL_B, tm, COL_B), lambda i, j: (0, i, 0)),
            pl.BlockSpec((key_dim // COL_B, tm, COL_B), lambda i, j: (0, i, 0)),
        ],
        out_shape=[
            jax.ShapeDtypeStruct((n_chunks * cw // COL_B, rows, COL_B), BF16),
            jax.ShapeDtypeStruct((key_dim // COL_B, rows, COL_B), F32),
            jax.ShapeDtypeStruct((key_dim // COL_B, rows, COL_B), F32),
        ],
        scratch_shapes=[pltpu.VMEM((tm, d), BF16),
                        pltpu.VMEM((tm, p["w_gl"].shape[0]), BF16)],
        compiler_params=pltpu.CompilerParams(
            dimension_semantics=("arbitrary", "arbitrary"),
            vmem_limit_bytes=VMEM_LIMIT),
        name="proj_b",
    )(x2, w_main, *small)


def _row_to_col(vec_row, width):
    n = vec_row.shape[1]
    return jnp.broadcast_to(vec_row, (width, n)).T


def _gla_macro_chunk(q, k, v_blocks, cum, state_blocks, reverse):
    c_rows, dk = q.shape
    ns = c_rows // SUB_B
    qs = q.astype(F32) * (dk ** -0.5)
    kf = k.astype(F32)
    order = list(range(ns))[::-1] if reverse else list(range(ns))
    mid_row = SUB_B // 2 - 1 if reverse else SUB_B // 2
    end_row = 0 if reverse else SUB_B - 1

    rsl = {a: slice(a * SUB_B, (a + 1) * SUB_B) for a in range(ns)}
    r_end, q_diag, k_diag, q_off, k_off = {}, {}, {}, {}, {}
    r_prev = {}
    zero_row = jnp.zeros((1, dk), F32)
    prev = zero_row
    for a in order:
        ca = cum[rsl[a]]
        mid = ca[mid_row:mid_row + 1]
        r_end[a] = ca[end_row:end_row + 1]
        r_prev[a] = prev
        q_diag[a] = (qs[rsl[a]] * jnp.exp2(ca - mid)).astype(BF16)
        k_diag[a] = (kf[rsl[a]] * jnp.exp2(mid - ca)).astype(BF16)
        q_off[a] = qs[rsl[a]] * jnp.exp2(ca - prev)
        k_off[a] = kf[rsl[a]] * jnp.exp2(r_end[a] - ca)
        prev = r_end[a]
    r_last = prev

    ri = lax.broadcasted_iota(jnp.int32, (SUB_B, SUB_B), 0)
    ci = lax.broadcasted_iota(jnp.int32, (SUB_B, SUB_B), 1)
    keep = (ci > ri) if reverse else (ci <= ri)

    k_off_b = {a: k_off[a].astype(BF16) for a in range(ns)}
    a_rows = []
    q_state = []
    for a in range(ns):
        blocks = []
        for b in range(ns):
            if b == a:
                blk = jnp.where(keep, _dot_nt(q_diag[a], k_diag[a]), 0.0)
            elif order.index(b) < order.index(a):
                if order.index(b) + 1 == order.index(a):
                    qa = q_off[a]
                else:
                    qa = q_off[a] * jnp.exp2(r_prev[a] - r_end[b])
                blk = _dot_nt(qa.astype(BF16), k_off_b[b])
            else:
                blk = jnp.zeros((SUB_B, SUB_B), F32)
            blocks.append(blk)
        a_rows.append(jnp.concatenate(blocks, axis=1) if ns > 1 else blocks[0])
        q_state.append(q_off[a] * jnp.exp2(r_prev[a]))
    a_full = jnp.concatenate(a_rows, axis=0) if ns > 1 else a_rows[0]
    q_st = jnp.concatenate(q_state, axis=0) if ns > 1 else q_state[0]

    a_b = a_full.astype(BF16)
    q_b = q_st.astype(BF16)
    k_dec = [k_off[a] * jnp.exp2(r_last - r_end[a]) for a in range(ns)]
    k_dec = (jnp.concatenate(k_dec, axis=0) if ns > 1 else k_dec[0]).astype(BF16)
    dcol = _row_to_col(jnp.exp2(r_last), 128)
    dcol = jnp.concatenate([dcol] * (v_blocks[0].shape[1] // 128), axis=1)
    o_blocks, new_state = [], []
    for vc, sc in zip(v_blocks, state_blocks):
        o_blocks.append(_dot(a_b, vc) + _dot(q_b, sc.astype(BF16)))
        new_state.append(sc * dcol + _dot_tn(k_dec, vc))
    return o_blocks, new_state


def _gla_sweep_tile(q_ref, k_ref, v_ref, cum_ref, s_ref, h, emit, reverse):
    tb = q_ref.shape[0]
    nvb = v_ref.shape[0]
    nm = tb // MACRO_B
    chunks = range(nm - 1, -1, -1) if reverse else range(nm)
    for m in chunks:
        rows = slice(m * MACRO_B, (m + 1) * MACRO_B)
        o_blocks, s_new = _gla_macro_chunk(
            q_ref[rows, :], k_ref[rows, :], [v_ref[c, rows, :] for c in range(nvb)], cum_ref[rows, :],
            [s_ref[h, c] for c in range(nvb)], reverse)
        for c in range(nvb):
            s_ref[h, c] = s_new[c]
        emit(rows, o_blocks)


def _gla_fwd_kernel(q_ref, k_ref, v_ref, cum_ref, o_ref, s_ref):
    i = pl.program_id(1)
    h = pl.program_id(2)

    @pl.when(i == 0)
    def _reset():
        s_ref[h] = jnp.zeros(s_ref.shape[1:], F32)

    def emit(rows, o_blocks):
        for c, o in enumerate(o_blocks):
            o_ref[c, rows, :] = o.astype(o_ref.dtype)

    _gla_sweep_tile(q_ref, k_ref, v_ref, cum_ref, s_ref, h, emit, reverse=False)


def _gla_dims(hmain, batch, key_dim, d_inner, tile):
    rows = hmain.shape[1]
    dk = key_dim // GLA_HEADS
    dv = d_inner // GLA_HEADS
    assert dk == COL_B and dv % COL_B == 0
    tb = min(tile, rows // batch)
    nt = rows // batch // tb
    nvb = dv // COL_B
    kk = key_dim // COL_B
    kv = 2 * key_dim // (nvb * COL_B)
    kz = (2 * key_dim + d_inner) // (nvb * COL_B)
    return rows, dk, dv, tb, nt, nvb, kk, kv, kz


def _gla_fwd(hmain, cum_f, batch, key_dim, d_inner):
    rows, dk, dv, tb, nt, nvb, kk, kv, _ = _gla_dims(hmain, batch, key_dim, d_inner, TB_GF)
    grid = (batch, nt, GLA_HEADS)
    return pl.pallas_call(
        _gla_fwd_kernel,
        grid=grid,
        in_specs=[
            pl.BlockSpec((None, tb, dk), lambda b, i, h: (h, b * nt + i, 0)),
            pl.BlockSpec((None, tb, dk), lambda b, i, h: (kk + h, b * nt + i, 0)),
            pl.BlockSpec((nvb, tb, COL_B), lambda b, i, h: (kv + h, b * nt + i, 0)),
            pl.BlockSpec((None, tb, dk), lambda b, i, h: (h, b * nt + i, 0)),
        ],
        out_specs=pl.BlockSpec((nvb, tb, COL_B), lambda b, i, h: (h, b * nt + i, 0)),
        out_shape=jax.ShapeDtypeStruct((GLA_HEADS * nvb, rows, COL_B), BF16),
        scratch_shapes=[pltpu.VMEM((GLA_HEADS, nvb, dk, COL_B), F32)],
        compiler_params=pltpu.CompilerParams(
            dimension_semantics=("arbitrary", "arbitrary", "arbitrary"),
            vmem_limit_bytes=VMEM_LIMIT),
        name="gla_fwd",
    )(hmain, hmain, hmain, cum_f)


def _gla_bwd_kernel(q_ref, k_ref, v_ref, cum_ref, of_ref, z_ref, x_ref, wo_ref, gn_ref,
                    lng_ref, lnb_ref, o_ref, s_ref, y_ref, *, alpha):
    i = pl.program_id(1)
    h = pl.program_id(2)
    tb = o_ref.shape[0]
    xr = x_ref.shape[0]

    @pl.when(i == 0)
    def _reset():
        s_ref[h] = jnp.zeros(s_ref.shape[1:], F32)

    @pl.when(h == 0)
    def _zero_acc():
        o_ref[...] = jnp.zeros_like(o_ref)

    def emit(rows, o_blocks):
        nvb = len(o_blocks)
        o = [of_ref[c, rows, :].astype(F32) + o_blocks[c] for c in range(nvb)]
        ss = sum(jnp.sum(oc * oc, axis=-1, keepdims=True) for oc in o)
        rstd = lax.rsqrt(ss * (1.0 / (nvb * COL_B)) + RMS_EPS)
        for c in range(nvb):
            cols = slice(c * COL_B, (c + 1) * COL_B)
            y_ref[rows, cols] = (o[c] * rstd * gn_ref[:, cols]
                                 * z_ref[c, rows, :].astype(F32)).astype(BF16)

    _gla_sweep_tile(q_ref, k_ref, v_ref, cum_ref, s_ref, h, emit, reverse=True)

    @pl.when(h < GLA_HEADS - 1)
    def _project():
        o_ref[...] += _dot(y_ref[...], wo_ref[...])
        xrows = pl.ds(pl.multiple_of(h * xr, xr), xr)
        o_ref[xrows, :] += alpha * x_ref[...]

    @pl.when(h == GLA_HEADS - 1)
    def _project_and_finish():
        for r in range(GLA_HEADS):
            rows = slice(r * xr, (r + 1) * xr)
            acc = o_ref[rows, :] + _dot(y_ref[rows, :], wo_ref[...])
            if r == GLA_HEADS - 1:
                acc = acc + alpha * x_ref[...]
            o_ref[rows, :] = _layer_norm_rows(acc, lng_ref[...], lnb_ref[...])


def _gla_bwd(hmain, cum_b, o_f, x2, p, lng, lnb, batch, alpha):
    d = x2.shape[1]
    w_out = p["w_out"]
    key_dim = p["key_dim"]
    d_inner = w_out.shape[0]
    rows, dk, dv, tb, nt, nvb, kk, kv, kz = _gla_dims(hmain, batch, key_dim, d_inner, TB_GB)
    grid = (batch, nt, GLA_HEADS)

    def row(b, i):
        return b * nt + (nt - 1 - i)

    xr = tb // GLA_HEADS
    kernel = functools.partial(_gla_bwd_kernel, alpha=alpha)
    return pl.pallas_call(
        kernel,
        grid=grid,
        in_specs=[
            pl.BlockSpec((None, tb, dk), lambda b, i, h: (h, row(b, i), 0)),
            pl.BlockSpec((None, tb, dk), lambda b, i, h: (kk + h, row(b, i), 0)),
            pl.BlockSpec((nvb, tb, COL_B), lambda b, i, h: (kv + h, row(b, i), 0)),
            pl.BlockSpec((None, tb, dk), lambda b, i, h: (h, row(b, i), 0)),
            pl.BlockSpec((nvb, tb, COL_B), lambda b, i, h: (h, row(b, i), 0)),
            pl.BlockSpec((nvb, tb, COL_B), lambda b, i, h: (kz + h, row(b, i), 0)),
            pl.BlockSpec((xr, d), lambda b, i, h: (row(b, i) * GLA_HEADS + h, 0)),
            pl.BlockSpec((dv, d), lambda b, i, h: (h, 0)),
            pl.BlockSpec((1, dv), lambda b, i, h: (0, h)),
            pl.BlockSpec((1, d), lambda b, i, h: (0, 0)),
            pl.BlockSpec((1, d), lambda b, i, h: (0, 0)),
        ],
        out_specs=pl.BlockSpec((tb, d), lambda b, i, h: (row(b, i), 0)),
        out_shape=jax.ShapeDtypeStruct((rows, d), F32),
        scratch_shapes=[
            pltpu.VMEM((GLA_HEADS, nvb, dk, COL_B), F32),
            pltpu.VMEM((tb, dv), BF16),
        ],
        compiler_params=pltpu.CompilerParams(
            dimension_semantics=("arbitrary", "arbitrary", "arbitrary"),
            vmem_limit_bytes=VMEM_LIMIT),
        name="gla_bwd",
    )(hmain, hmain, hmain, cum_b, o_f, hmain, x2, w_out, p["gn"], lng, lnb)


def _tri_blocks(reverse):
    r = jnp.arange(CUM_ROWS)[:, None]
    c = jnp.arange(CUM_ROWS)[None, :]
    same = (r // MACRO_B) == (c // MACRO_B)
    tri = (c >= r) if reverse else (c <= r)
    return (same & tri).astype(BF16)


def _cast_block_kernel(w_ref, o_ref):
    o_ref[...] = w_ref[...].astype(BF16)


def _relayout_cols(w, cw, n_slots, src_chunk):
    k = w.shape[0]
    return pl.pallas_call(
        _cast_block_kernel,
        grid=(n_slots,),
        in_specs=[pl.BlockSpec((k, cw), lambda s: (0, src_chunk(s)))],
        out_specs=pl.BlockSpec((None, k, cw), lambda s: (s, 0, 0)),
        out_shape=jax.ShapeDtypeStruct((n_slots, k, cw), BF16),
        compiler_params=pltpu.CompilerParams(dimension_semantics=("arbitrary",)),
        name="weight_relayout",
    )(w)


def _prep_a(w_in3, layer, vg, vb, w_s, b_s, w_out):
    d = w_in3.shape[1]
    d_inner = w_out.shape[0]
    nc = d_inner // CW_A
    npv = nc // 2

    def src_chunk(s):
        pair, t = s // 2, s % 2
        return jnp.where(pair < npv, nc + s, jnp.where(t == 0, pair - npv, 2 * nc + pair - npv))

    w1 = _relayout_cols(w_in3[layer], CW_A, 3 * nc, src_chunk).reshape(npv + nc, 2, d, CW_A)
    return dict(w1=w1, w_s=w_s.astype(BF16),
                b_s=b_s[:, :, None].astype(F32), w_out=w_out.astype(BF16),
                vg=vg[None, :], vb=vb[None, :])


def _prep_b(w_in3, layer, w_g2, b_g, gn_g, w_out):
    key_dim = w_g2.shape[2]
    d_inner = w_out.shape[0]
    n_main = 2 * key_dim + 2 * d_inner
    zeros = jnp.zeros((GATE_RANK, key_dim), F32)
    w_gate = jnp.concatenate([jnp.concatenate([w_g2[0], zeros], axis=1),
                              jnp.concatenate([zeros, w_g2[1]], axis=1)], axis=0)
    return dict(w_main=_relayout_cols(w_in3[layer], CW_P, n_main // CW_P, lambda s: s),
                w_gl=w_in3[layer, :, n_main:].T.astype(BF16),
                w_gate=w_gate.astype(BF16), b_gate=b_g.reshape(1, 2 * key_dim),
                gn=gn_g[None, :], w_out=w_out.astype(BF16), key_dim=key_dim, d_inner=d_inner,
                tri_f=_tri_blocks(False), tri_b=_tri_blocks(True))


def _mixer_b_layer(x2, batch, p, lng, lnb, alpha):
    hmain, cum_f, cum_b = _proj_b(x2, p)
    o_f = _gla_fwd(hmain, cum_f, batch, p["key_dim"], p["d_inner"])
    return _gla_bwd(hmain, cum_b, o_f, x2, p, lng, lnb, batch, alpha)


def _trunk(x, layers, ln_g, ln_b, alpha):
    batch, seq, d = x.shape
    x2 = x.reshape(batch * seq, d)
    for i, (kind, p) in enumerate(layers):
        lng, lnb = ln_g[i][None, :], ln_b[i][None, :]
        if kind == "a":
            x2 = _layer_a(x2, p, lng, lnb, alpha)
        else:
            x2 = _mixer_b_layer(x2, batch, p, lng, lnb, alpha)
    return x2.reshape(batch, seq, d)


def kernel(x_prompt, x_sample, w_in_a, ln_v_g_a, ln_v_b_a, w_s_a, b_s_a, w_out_a, w_in_b, w_g2_b, b_g_b, gn_g_b, w_out_b, ln_g, ln_b):
    depth = ln_g.shape[0]
    alpha = (2 * depth) ** 0.25
    layers = []
    for i in range(depth):
        j = i // 2
        if i % 2 == 0:
            layers.append(("a", _prep_a(w_in_a, j, ln_v_g_a[j], ln_v_b_a[j], w_s_a[j], b_s_a[j], w_out_a[j])))
        else:
            layers.append(("b", _prep_b(w_in_b, j, w_g2_b[j], b_g_b[j], gn_g_b[j], w_out_b[j])))
    y_prompt = _trunk(x_prompt, layers, ln_g, ln_b, alpha)
    y_sample = _trunk(x_sample, layers, ln_g, ln_b, alpha)
    return (y_prompt, y_sample)
```

```python
import functools

import jax
import jax.numpy as jnp
from jax import lax
from jax.experimental import pallas as pl
from jax.experimental.pallas import tpu as pltpu

F32 = jnp.float32
BF16 = jnp.bfloat16

CHUNK_A = 128
N_GROUPS_A = 16
GLA_HEADS = 4
GATE_RANK = 16
GATE_TAU = 16.0
LOG2_E = 1.4426950408889634
SUB_B = 64
LN_EPS = 1e-5
RMS_EPS = 1e-6

TM_A = 1024
CW_A = 512
RB_A = 256
MXU_COLS = 256
TM_P = 1024
CW_P = 1024
TB_GF = 2048
TB_GB = 1024
MACRO_B = 256
CUM_ROWS = 256
COL_B = 256
VMEM_LIMIT = 60 * 1024 * 1024


def _dot(a, b):
    return jnp.dot(a, b, preferred_element_type=F32)


def _dot_nt(a, b):
    return lax.dot_general(a, b, (((1,), (1,)), ((), ())), preferred_element_type=F32)


def _dot_tn(a, b):
    return lax.dot_general(a, b, (((0,), (0,)), ((), ())), preferred_element_type=F32)


def _gelu_tanh(x):
    c = 0.7978845608028654
    hx = 0.5 * x
    return hx + hx * jnp.tanh(x * (c + (c * 0.044715) * (x * x)))


def _silu(x):
    hx = 0.5 * x
    return hx + hx * jnp.tanh(hx)


def _log_sigmoid(x):
    return jnp.minimum(x, 0.0) - jnp.log(1.0 + jnp.exp(-jnp.abs(x)))


def _layer_norm_rows(h, g, b):
    mu = jnp.mean(h, axis=-1, keepdims=True)
    hc = h - mu
    var = jnp.mean(hc * hc, axis=-1, keepdims=True)
    return hc * lax.rsqrt(var + LN_EPS) * g + b


def _layer_a_kernel(x_hbm, w1_ref, wo_ref, ws_ref, bs_ref, vg_ref, vb_ref,
                    lng_ref, lnb_ref, o_ref,
                    stage_ref, xb_ref, gv_ref, raw_ref, sum_ref, sq_ref, mu_ref, rstd_ref, y_ref, sem,
                    *, nc, d_inner, alpha, n_tiles):
    i = pl.program_id(0)
    j = pl.program_id(1)
    tm = o_ref.shape[0]
    cw = w1_ref.shape[2]
    gdim = d_inner // N_GROUPS_A
    npv = nc // 2
    lanes = sum_ref.shape[1]

    def x_copy(tile):
        return pltpu.make_async_copy(x_hbm.at[pl.ds(tile * tm, tm), :], stage_ref, sem)

    @pl.when((j == 0) & (i == 0))
    def _first_fetch():
        x_copy(0).start()

    def gelu_and_stats(chunk, v):
        for pc in range(cw // MXU_COLS):
            cols = slice(pc * MXU_COLS, (pc + 1) * MXU_COLS)
            gv = _gelu_tanh(v[:, cols])
            gv_ref[chunk, :, cols] = gv.astype(BF16)
            part = gv[:, 0:lanes]
            part2 = part * part
            for l in range(1, MXU_COLS // lanes):
                blk = gv[:, l * lanes:(l + 1) * lanes]
                part = part + blk
                part2 = part2 + blk * blk
            sum_ref[...] += part
            sq_ref[...] += part2

    def phase_v_body(first):
        if not first:
            gelu_and_stats(2 * j - 1, raw_ref[...])
        gelu_and_stats(2 * j, _dot(xb_ref[...], w1_ref[0]))
        raw_ref[...] = _dot(xb_ref[...], w1_ref[1])

    @pl.when(j == 0)
    def _first_step():
        x_copy(i).wait()
        for r in range(tm // RB_A):
            rows = slice(r * RB_A, (r + 1) * RB_A)
            x = stage_ref[rows, :]
            xb_ref[rows, :] = x.astype(BF16)
            o_ref[rows, :] = alpha * x
        sum_ref[...] = jnp.zeros_like(sum_ref)
        sq_ref[...] = jnp.zeros_like(sq_ref)
        phase_v_body(first=True)

    @pl.when((j == 1) & (i + 1 < n_tiles))
    def _prefetch():
        x_copy(i + 1).start()

    @pl.when((j > 0) & (j < npv))
    def _phase_v():
        phase_v_body(first=False)

    def finish_stats():
        gelu_and_stats(nc - 1, raw_ref[...])
        mu = jnp.sum(sum_ref[...], axis=1, keepdims=True) * (1.0 / d_inner)
        var = jnp.sum(sq_ref[...], axis=1, keepdims=True) * (1.0 / d_inner) - mu * mu
        mu_ref[...] = mu
        rstd_ref[...] = lax.rsqrt(var + LN_EPS)

    def mix_body(first=False):
        c = j - npv
        if first:
            finish_stats()
        xb = xb_ref[...]
        gate = _gelu_tanh(_dot(xb, w1_ref[0])) * _silu(_dot(xb, w1_ref[1]))
        vn = ((gv_ref[c].astype(F32) - mu_ref[...]) * rstd_ref[...] * vg_ref[...]
              + vb_ref[...]).astype(BF16)
        for g in range(cw // gdim):
            cols = slice(g * gdim, (g + 1) * gdim)
            wsg = ws_ref[g]
            bsg = jnp.broadcast_to(bs_ref[g], (CHUNK_A, gdim))
            for n in range(tm // CHUNK_A):
                rows = slice(n * CHUNK_A, (n + 1) * CHUNK_A)
                s = _dot(wsg, vn[rows, cols]) + bsg
                y_ref[rows, cols] = (gate[rows, cols] * s).astype(BF16)
        o_ref[...] += _dot(y_ref[...], wo_ref[...])

    @pl.when(j == npv)
    def _first_mix():
        mix_body(first=True)

    @pl.when(j > npv)
    def _mix():
        mix_body()

    @pl.when(j == npv + nc - 1)
    def _finish():
        for r in range(tm // RB_A):
            rows = slice(r * RB_A, (r + 1) * RB_A)
            o_ref[rows, :] = _layer_norm_rows(o_ref[rows, :], lng_ref[...], lnb_ref[...])


def _layer_a(x2, p, lng, lnb, alpha):
    rows, d = x2.shape
    w1, w_out = p["w1"], p["w_out"]
    d_inner = w_out.shape[0]
    cw = w1.shape[3]
    nc = d_inner // cw
    npv = nc // 2
    gpc = cw // (d_inner // N_GROUPS_A)
    tm = min(TM_A, rows)
    grid = (rows // tm, npv + nc)

    def cmix(j):
        return jnp.maximum(j - npv, 0)

    kernel = functools.partial(_layer_a_kernel, nc=nc, d_inner=d_inner, alpha=alpha, n_tiles=grid[0])
    return pl.pallas_call(
        kernel,
        grid=grid,
        in_specs=[
            pl.BlockSpec(memory_space=pl.ANY),
            pl.BlockSpec((None, 2, d, cw), lambda i, j: (j, 0, 0, 0)),
            pl.BlockSpec((cw, d), lambda i, j: (cmix(j), 0)),
            pl.BlockSpec((gpc, CHUNK_A, CHUNK_A), lambda i, j: (cmix(j), 0, 0)),
            pl.BlockSpec((gpc, CHUNK_A, 1), lambda i, j: (cmix(j), 0, 0)),
            pl.BlockSpec((1, cw), lambda i, j: (0, cmix(j))),
            pl.BlockSpec((1, cw), lambda i, j: (0, cmix(j))),
            pl.BlockSpec((1, d), lambda i, j: (0, 0)),
            pl.BlockSpec((1, d), lambda i, j: (0, 0)),
        ],
        out_specs=pl.BlockSpec((tm, d), lambda i, j: (i, 0)),
        out_shape=jax.ShapeDtypeStruct((rows, d), F32),
        scratch_shapes=[
            pltpu.VMEM((tm, d), F32),
            pltpu.VMEM((tm, d), BF16),
            pltpu.VMEM((nc, tm, cw), BF16),
            pltpu.VMEM((tm, cw), F32),
            pltpu.VMEM((tm, 128), F32),
            pltpu.VMEM((tm, 128), F32),
            pltpu.VMEM((tm, 1), F32),
            pltpu.VMEM((tm, 1), F32),
            pltpu.VMEM((tm, cw), BF16),
            pltpu.SemaphoreType.DMA,
        ],
        compiler_params=pltpu.CompilerParams(
            dimension_semantics=("arbitrary", "arbitrary"),
            vmem_limit_bytes=VMEM_LIMIT),
        name="layer_a",
    )(x2, w1, w_out, p["w_s"], p["b_s"], p["vg"], p["vb"], lng, lnb)


def _proj_b_kernel(x_ref, w_ref, wgl_ref, wg_ref, bg_ref, lf_ref, lb_ref,
                   h_ref, gf_ref, rb_ref, xb_ref, gl_ref, *, key_dim, z_chunk0):
    j = pl.program_id(1)
    tm = x_ref.shape[0]

    def project(gate_act=False):
        res = _dot(xb_ref[...], w_ref[...])
        for c in range(h_ref.shape[0]):
            blk = res[:, c * COL_B:(c + 1) * COL_B]
            h_ref[c] = (_silu(blk) if gate_act else blk).astype(BF16)

    def gates(r):
        rows = slice(r * CUM_ROWS, (r + 1) * CUM_ROWS)
        pre = _dot(gl_ref[rows, :], wg_ref[...]) + bg_ref[...]
        glog = _log_sigmoid(pre) * (LOG2_E / GATE_TAU)
        for dst, tri, cols in ((gf_ref, lf_ref, slice(0, key_dim)),
                               (rb_ref, lb_ref, slice(key_dim, 2 * key_dim))):
            gb = glog[:, cols]
            hi = gb.astype(BF16)
            lo = (gb - hi.astype(F32)).astype(BF16)
            cs = _dot(tri[...], hi) + _dot(tri[...], lo)
            for c in range(key_dim // COL_B):
                dst[c, rows, :] = cs[:, c * COL_B:(c + 1) * COL_B]

    @pl.when(j == 0)
    def _gates_and_first_chunk():
        for r in range(tm // CUM_ROWS):
            rows = slice(r * CUM_ROWS, (r + 1) * CUM_ROWS)
            xb_ref[rows, :] = x_ref[rows, :].astype(BF16)
        gl_t = _dot_nt(wgl_ref[...], xb_ref[...])
        gl_ref[...] = gl_t.T.astype(BF16)
        for r in range(tm // CUM_ROWS):
            gates(r)
        project()

    @pl.when((j > 0) & (j < z_chunk0))
    def _qkv_chunks():
        project()

    @pl.when(j >= z_chunk0)
    def _z_chunks():
        project(gate_act=True)


def _proj_b(x2, p):
    rows, d = x2.shape
    w_main = p["w_main"]
    n_chunks, _, cw = w_main.shape
    key_dim = p["key_dim"]
    tm = min(TM_P, rows)
    grid = (rows // tm, n_chunks)
    z_chunk0 = (2 * key_dim + p["d_inner"]) // cw
    assert z_chunk0 * cw == 2 * key_dim + p["d_inner"] and z_chunk0 > 0
    kernel = functools.partial(_proj_b_kernel, key_dim=key_dim, z_chunk0=z_chunk0)
    small = [p["w_gl"], p["w_gate"], p["b_gate"], p["tri_f"], p["tri_b"]]
    return pl.pallas_call(
        kernel,
        grid=grid,
        in_specs=[
            pl.BlockSpec((tm, d), lambda i, j: (i, 0)),
            pl.BlockSpec((None, d, cw), lambda i, j: (j, 0, 0)),
        ] + [pl.BlockSpec(a.shape, lambda i, j: (0, 0)) for a in small],
        out_specs=[
            pl.BlockSpec((cw // COL_B, tm, COL_B), lambda i, j: (j, i, 0)),
            pl.BlockSpec((key_dim // COL_B, tm, COL_B), lambda i, j: (0, i, 0)),
            pl.BlockSpec((key_dim // COL_B, tm, COL_B), lambda i, j: (0, i, 0)),
        ],
        out_shape=[
            jax.ShapeDtypeStruct((n_chunks * cw // COL_B, rows, COL_B), BF16),
            jax.ShapeDtypeStruct((key_dim // COL_B, rows, COL_B), F32),
            jax.ShapeDtypeStruct((key_dim // COL_B, rows, COL_B), F32),
        ],
        scratch_shapes=[pltpu.VMEM((tm, d), BF16),
                        pltpu.VMEM((tm, p["w_gl"].shape[0]), BF16)],
        compiler_params=pltpu.CompilerParams(
            dimension_semantics=("arbitrary", "arbitrary"),
            vmem_limit_bytes=VMEM_LIMIT),
        name="proj_b",
    )(x2, w_main, *small)


def _row_to_col(vec_row, width):
    n = vec_row.shape[1]
    return jnp.broadcast_to(vec_row, (width, n)).T


def _gla_macro_chunk(q, k, v_blocks, cum, state_blocks, reverse):
    c_rows, dk = q.shape
    ns = c_rows // SUB_B
    qs = q.astype(F32) * (dk ** -0.5)
    kf = k.astype(F32)
    order = list(range(ns))[::-1] if reverse else list(range(ns))
    mid_row = SUB_B // 2 - 1 if reverse else SUB_B // 2
    end_row = 0 if reverse else SUB_B - 1

    rsl = {a: slice(a * SUB_B, (a + 1) * SUB_B) for a in range(ns)}
    r_end, q_diag, k_diag, q_off, k_off = {}, {}, {}, {}, {}
    r_prev = {}
    zero_row = jnp.zeros((1, dk), F32)
    prev = zero_row
    for a in order:
        ca = cum[rsl[a]]
        mid = ca[mid_row:mid_row + 1]
        r_end[a] = ca[end_row:end_row + 1]
        r_prev[a] = prev
        q_diag[a] = (qs[rsl[a]] * jnp.exp2(ca - mid)).astype(BF16)
        k_diag[a] = (kf[rsl[a]] * jnp.exp2(mid - ca)).astype(BF16)
        q_off[a] = qs[rsl[a]] * jnp.exp2(ca - prev)
        k_off[a] = kf[rsl[a]] * jnp.exp2(r_end[a] - ca)
        prev = r_end[a]
    r_last = prev

    ri = lax.broadcasted_iota(jnp.int32, (SUB_B, SUB_B), 0)
    ci = lax.broadcasted_iota(jnp.int32, (SUB_B, SUB_B), 1)
    keep = (ci > ri) if reverse else (ci <= ri)

    k_off_b = {a: k_off[a].astype(BF16) for a in range(ns)}
    a_rows = []
    q_state = []
    for a in range(ns):
        blocks = []
        for b in range(ns):
            if b == a:
                blk = jnp.where(keep, _dot_nt(q_diag[a], k_diag[a]), 0.0)
            elif order.index(b) < order.index(a):
                if order.index(b) + 1 == order.index(a):
                    qa = q_off[a]
                else:
                    qa = q_off[a] * jnp.exp2(r_prev[a] - r_end[b])
                blk = _dot_nt(qa.astype(BF16), k_off_b[b])
            else:
                blk = jnp.zeros((SUB_B, SUB_B), F32)
            blocks.append(blk)
        a_rows.append(jnp.concatenate(blocks, axis=1) if ns > 1 else blocks[0])
        q_state.append(q_off[a] * jnp.exp2(r_prev[a]))
    a_full = jnp.concatenate(a_rows, axis=0) if ns > 1 else a_rows[0]
    q_st = jnp.concatenate(q_state, axis=0) if ns > 1 else q_state[0]

    a_b = a_full.astype(BF16)
    q_b = q_st.astype(BF16)
    k_dec = [k_off[a] * jnp.exp2(r_last - r_end[a]) for a in range(ns)]
    k_dec = (jnp.concatenate(k_dec, axis=0) if ns > 1 else k_dec[0]).astype(BF16)
    dcol = _row_to_col(jnp.exp2(r_last), 128)
    dcol = jnp.concatenate([dcol] * (v_blocks[0].shape[1] // 128), axis=1)
    o_blocks, new_state = [], []
    for vc, sc in zip(v_blocks, state_blocks):
        o_blocks.append(_dot(a_b, vc) + _dot(q_b, sc.astype(BF16)))
        new_state.append(sc * dcol + _dot_tn(k_dec, vc))
    return o_blocks, new_state


def _gla_sweep_tile(q_ref, k_ref, v_ref, cum_ref, s_ref, h, emit, reverse):
    tb = q_ref.shape[0]
    nvb = v_ref.shape[0]
    nm = tb // MACRO_B
    chunks = range(nm - 1, -1, -1) if reverse else range(nm)
    for m in chunks:
        rows = slice(m * MACRO_B, (m + 1) * MACRO_B)
        o_blocks, s_new = _gla_macro_chunk(
            q_ref[rows, :], k_ref[rows, :], [v_ref[c, rows, :] for c in range(nvb)], cum_ref[rows, :],
            [s_ref[h, c] for c in range(nvb)], reverse)
        for c in range(nvb):
            s_ref[h, c] = s_new[c]
        emit(rows, o_blocks)


def _gla_fwd_kernel(q_ref, k_ref, v_ref, cum_ref, o_ref, s_ref):
    i = pl.program_id(1)
    h = pl.program_id(2)

    @pl.when(i == 0)
    def _reset():
        s_ref[h] = jnp.zeros(s_ref.shape[1:], F32)

    def emit(rows, o_blocks):
        for c, o in enumerate(o_blocks):
            o_ref[c, rows, :] = o.astype(o_ref.dtype)

    _gla_sweep_tile(q_ref, k_ref, v_ref, cum_ref, s_ref, h, emit, reverse=False)


def _gla_dims(hmain, batch, key_dim, d_inner, tile):
    rows = hmain.shape[1]
    dk = key_dim // GLA_HEADS
    dv = d_inner // GLA_HEADS
    assert dk == COL_B and dv % COL_B == 0
    tb = min(tile, rows // batch)
    nt = rows // batch // tb
    nvb = dv // COL_B
    kk = key_dim // COL_B
    kv = 2 * key_dim // (nvb * COL_B)
    kz = (2 * key_dim + d_inner) // (nvb * COL_B)
    return rows, dk, dv, tb, nt, nvb, kk, kv, kz


def _gla_fwd(hmain, cum_f, batch, key_dim, d_inner):
    rows, dk, dv, tb, nt, nvb, kk, kv, _ = _gla_dims(hmain, batch, key_dim, d_inner, TB_GF)
    grid = (batch, nt, GLA_HEADS)
    return pl.pallas_call(
        _gla_fwd_kernel,
        grid=grid,
        in_specs=[
            pl.BlockSpec((None, tb, dk), lambda b, i, h: (h, b * nt + i, 0)),
            pl.BlockSpec((None, tb, dk), lambda b, i, h: (kk + h, b * nt + i, 0)),
            pl.BlockSpec((nvb, tb, COL_B), lambda b, i, h: (kv + h, b * nt + i, 0)),
            pl.BlockSpec((None, tb, dk), lambda b, i, h: (h, b * nt + i, 0)),
        ],
        out_specs=pl.BlockSpec((nvb, tb, COL_B), lambda b, i, h: (h, b * nt + i, 0)),
        out_shape=jax.ShapeDtypeStruct((GLA_HEADS * nvb, rows, COL_B), BF16),
        scratch_shapes=[pltpu.VMEM((GLA_HEADS, nvb, dk, COL_B), F32)],
        compiler_params=pltpu.CompilerParams(
            dimension_semantics=("arbitrary", "arbitrary", "arbitrary"),
            vmem_limit_bytes=VMEM_LIMIT),
        name="gla_fwd",
    )(hmain, hmain, hmain, cum_f)


def _gla_bwd_kernel(q_ref, k_ref, v_ref, cum_ref, of_ref, z_ref, x_ref, wo_ref, gn_ref,
                    lng_ref, lnb_ref, o_ref, s_ref, y_ref, *, alpha):
    i = pl.program_id(1)
    h = pl.program_id(2)
    tb = o_ref.shape[0]
    xr = x_ref.shape[0]

    @pl.when(i == 0)
    def _reset():
        s_ref[h] = jnp.zeros(s_ref.shape[1:], F32)

    @pl.when(h == 0)
    def _zero_acc():
        o_ref[...] = jnp.zeros_like(o_ref)

    def emit(rows, o_blocks):
        nvb = len(o_blocks)
        o = [of_ref[c, rows, :].astype(F32) + o_blocks[c] for c in range(nvb)]
        ss = sum(jnp.sum(oc * oc, axis=-1, keepdims=True) for oc in o)
        rstd = lax.rsqrt(ss * (1.0 / (nvb * COL_B)) + RMS_EPS)
        for c in range(nvb):
            cols = slice(c * COL_B, (c + 1) * COL_B)
            y_ref[rows, cols] = (o[c] * rstd * gn_ref[:, cols]
                                 * z_ref[c, rows, :].astype(F32)).astype(BF16)

    _gla_sweep_tile(q_ref, k_ref, v_ref, cum_ref, s_ref, h, emit, reverse=True)

    o_ref[...] += _dot(y_ref[...], wo_ref[...])
    xrows = pl.ds(pl.multiple_of(h * xr, xr), xr)
    o_ref[xrows, :] += alpha * x_ref[...]

    @pl.when(h == GLA_HEADS - 1)
    def _finish():
        for r in range(tb // CUM_ROWS):
            rows = slice(r * CUM_ROWS, (r + 1) * CUM_ROWS)
            o_ref[rows, :] = _layer_norm_rows(o_ref[rows, :], lng_ref[...], lnb_ref[...])


def _gla_bwd(hmain, cum_b, o_f, x2, p, lng, lnb, batch, alpha):
    d = x2.shape[1]
    w_out = p["w_out"]
    key_dim = p["key_dim"]
    d_inner = w_out.shape[0]
    rows, dk, dv, tb, nt, nvb, kk, kv, kz = _gla_dims(hmain, batch, key_dim, d_inner, TB_GB)
    grid = (batch, nt, GLA_HEADS)

    def row(b, i):
        return b * nt + (nt - 1 - i)

    xr = tb // GLA_HEADS
    kernel = functools.partial(_gla_bwd_kernel, alpha=alpha)
    return pl.pallas_call(
        kernel,
        grid=grid,
        in_specs=[
            pl.BlockSpec((None, tb, dk), lambda b, i, h: (h, row(b, i), 0)),
            pl.BlockSpec((None, tb, dk), lambda b, i, h: (kk + h, row(b, i), 0)),
            pl.BlockSpec((nvb, tb, COL_B), lambda b, i, h: (kv + h, row(b, i), 0)),
            pl.BlockSpec((None, tb, dk), lambda b, i, h: (h, row(b, i), 0)),
            pl.BlockSpec((nvb, tb, COL_B), lambda b, i, h: (h, row(b, i), 0)),
            pl.BlockSpec((nvb, tb, COL_B), lambda b, i, h: (kz + h, row(b, i), 0)),
            pl.BlockSpec((xr, d), lambda b, i, h: (row(b, i) * GLA_HEADS + h, 0)),
            pl.BlockSpec((dv, d), lambda b, i, h: (h, 0)),
            pl.BlockSpec((1, dv), lambda b, i, h: (0, h)),
            pl.BlockSpec((1, d), lambda b, i, h: (0, 0)),
            pl.BlockSpec((1, d), lambda b, i, h: (0, 0)),
        ],
        out_specs=pl.BlockSpec((tb, d), lambda b, i, h: (row(b, i), 0)),
        out_shape=jax.ShapeDtypeStruct((rows, d), F32),
        scratch_shapes=[
            pltpu.VMEM((GLA_HEADS, nvb, dk, COL_B), F32),
            pltpu.VMEM((tb, dv), BF16),
        ],
        compiler_params=pltpu.CompilerParams(
            dimension_semantics=("arbitrary", "arbitrary", "arbitrary"),
            vmem_limit_bytes=VMEM_LIMIT),
        name="gla_bwd",
    )(hmain, hmain, hmain, cum_b, o_f, hmain, x2, w_out, p["gn"], lng, lnb)


def _tri_blocks(reverse):
    r = jnp.arange(CUM_ROWS)[:, None]
    c = jnp.arange(CUM_ROWS)[None, :]
    same = (r // MACRO_B) == (c // MACRO_B)
    tri = (c >= r) if reverse else (c <= r)
    return (same & tri).astype(BF16)


def _cast_block_kernel(w_ref, o_ref):
    o_ref[...] = w_ref[...].astype(BF16)


def _relayout_cols(w, cw, n_slots, src_chunk):
    k = w.shape[0]
    return pl.pallas_call(
        _cast_block_kernel,
        grid=(n_slots,),
        in_specs=[pl.BlockSpec((k, cw), lambda s: (0, src_chunk(s)))],
        out_specs=pl.BlockSpec((None, k, cw), lambda s: (s, 0, 0)),
        out_shape=jax.ShapeDtypeStruct((n_slots, k, cw), BF16),
        compiler_params=pltpu.CompilerParams(dimension_semantics=("arbitrary",)),
        name="weight_relayout",
    )(w)


def _prep_a(w_in3, layer, vg, vb, w_s, b_s, w_out):
    d = w_in3.shape[1]
    d_inner = w_out.shape[0]
    nc = d_inner // CW_A
    npv = nc // 2

    def src_chunk(s):
        pair, t = s // 2, s % 2
        return jnp.where(pair < npv, nc + s, jnp.where(t == 0, pair - npv, 2 * nc + pair - npv))

    w1 = _relayout_cols(w_in3[layer], CW_A, 3 * nc, src_chunk).reshape(npv + nc, 2, d, CW_A)
    return dict(w1=w1, w_s=w_s.astype(BF16),
                b_s=b_s[:, :, None].astype(F32), w_out=w_out.astype(BF16),
                vg=vg[None, :], vb=vb[None, :])


def _prep_b(w_in3, layer, w_g2, b_g, gn_g, w_out):
    key_dim = w_g2.shape[2]
    d_inner = w_out.shape[0]
    n_main = 2 * key_dim + 2 * d_inner
    zeros = jnp.zeros((GATE_RANK, key_dim), F32)
    w_gate = jnp.concatenate([jnp.concatenate([w_g2[0], zeros], axis=1),
                              jnp.concatenate([zeros, w_g2[1]], axis=1)], axis=0)
    return dict(w_main=_relayout_cols(w_in3[layer], CW_P, n_main // CW_P, lambda s: s),
                w_gl=w_in3[layer, :, n_main:].T.astype(BF16),
                w_gate=w_gate.astype(BF16), b_gate=b_g.reshape(1, 2 * key_dim),
                gn=gn_g[None, :], w_out=w_out.astype(BF16), key_dim=key_dim, d_inner=d_inner,
                tri_f=_tri_blocks(False), tri_b=_tri_blocks(True))


def _mixer_b_layer(x2, batch, p, lng, lnb, alpha):
    hmain, cum_f, cum_b = _proj_b(x2, p)
    o_f = _gla_fwd(hmain, cum_f, batch, p["key_dim"], p["d_inner"])
    return _gla_bwd(hmain, cum_b, o_f, x2, p, lng, lnb, batch, alpha)


def _trunk(x, layers, ln_g, ln_b, alpha):
    batch, seq, d = x.shape
    x2 = x.reshape(batch * seq, d)
    for i, (kind, p) in enumerate(layers):
        lng, lnb = ln_g[i][None, :], ln_b[i][None, :]
        if kind == "a":
            x2 = _layer_a(x2, p, lng, lnb, alpha)
        else:
            x2 = _mixer_b_layer(x2, batch, p, lng, lnb, alpha)
    return x2.reshape(batch, seq, d)


def kernel(x_prompt, x_sample, w_in_a, ln_v_g_a, ln_v_b_a, w_s_a, b_s_a, w_out_a, w_in_b, w_g2_b, b_g_b, gn_g_b, w_out_b, ln_g, ln_b):
    depth = ln_g.shape[0]
    alpha = (2 * depth) ** 0.25
    layers = []
    for i in range(depth):
        j = i // 2
        if i % 2 == 0:
            layers.append(("a", _prep_a(w_in_a, j, ln_v_g_a[j], ln_v_b_a[j], w_s_a[j], b_s_a[j], w_out_a[j])))
        else:
            layers.append(("b", _prep_b(w_in_b, j, w_g2_b[j], b_g_b[j], gn_g_b[j], w_out_b[j])))
    y_prompt = _trunk(x_prompt, layers, ln_g, ln_b, alpha)
    y_sample = _trunk(x_sample, layers, ln_g, ln_b, alpha)
    return (y_prompt, y_sample)
```

```python
import functools

import jax
import jax.numpy as jnp
from jax import lax
from jax.experimental import pallas as pl
from jax.experimental.pallas import tpu as pltpu

F32 = jnp.float32
BF16 = jnp.bfloat16

CHUNK_A = 128
N_GROUPS_A = 16
GLA_HEADS = 4
GATE_RANK = 16
GATE_TAU = 16.0
LOG2_E = 1.4426950408889634
SUB_B = 64
LN_EPS = 1e-5
RMS_EPS = 1e-6

TM_A = 1024
CW_A = 512
RB_A = 256
MXU_COLS = 256
TM_P = 1024
CW_P = 1024
TB_GF = 4096
TB_GB = 1024
MACRO_B = 256
CUM_ROWS = 256
COL_B = 256
VMEM_LIMIT = 60 * 1024 * 1024


def _dot(a, b):
    return jnp.dot(a, b, preferred_element_type=F32)


def _dot_nt(a, b):
    return lax.dot_general(a, b, (((1,), (1,)), ((), ())), preferred_element_type=F32)


def _dot_tn(a, b):
    return lax.dot_general(a, b, (((0,), (0,)), ((), ())), preferred_element_type=F32)


def _gelu_tanh(x):
    c = 0.7978845608028654
    hx = 0.5 * x
    return hx + hx * jnp.tanh(x * (c + (c * 0.044715) * (x * x)))


def _silu(x):
    hx = 0.5 * x
    return hx + hx * jnp.tanh(hx)


def _log_sigmoid(x):
    return jnp.minimum(x, 0.0) - jnp.log(1.0 + jnp.exp(-jnp.abs(x)))


def _layer_norm_rows(h, g, b):
    mu = jnp.mean(h, axis=-1, keepdims=True)
    hc = h - mu
    var = jnp.mean(hc * hc, axis=-1, keepdims=True)
    return hc * lax.rsqrt(var + LN_EPS) * g + b


def _layer_a_kernel(x_hbm, w1_ref, wo_ref, ws_ref, bs_ref, vg_ref, vb_ref,
                    lng_ref, lnb_ref, o_ref,
                    stage_ref, xb_ref, gv_ref, raw_ref, sum_ref, sq_ref, mu_ref, rstd_ref, y_ref, sem,
                    *, nc, d_inner, alpha, n_tiles):
    i = pl.program_id(0)
    j = pl.program_id(1)
    tm = o_ref.shape[0]
    cw = w1_ref.shape[2]
    gdim = d_inner // N_GROUPS_A
    npv = nc // 2
    lanes = sum_ref.shape[1]

    def x_copy(tile):
        return pltpu.make_async_copy(x_hbm.at[pl.ds(tile * tm, tm), :], stage_ref, sem)

    @pl.when((j == 0) & (i == 0))
    def _first_fetch():
        x_copy(0).start()

    def gelu_and_stats(chunk, v):
        for pc in range(cw // MXU_COLS):
            cols = slice(pc * MXU_COLS, (pc + 1) * MXU_COLS)
            gv = _gelu_tanh(v[:, cols])
            gv_ref[chunk, :, cols] = gv.astype(BF16)
            part = gv[:, 0:lanes]
            part2 = part * part
            for l in range(1, MXU_COLS // lanes):
                blk = gv[:, l * lanes:(l + 1) * lanes]
                part = part + blk
                part2 = part2 + blk * blk
            sum_ref[...] += part
            sq_ref[...] += part2

    def phase_v_body(first):
        if not first:
            gelu_and_stats(2 * j - 1, raw_ref[...])
        gelu_and_stats(2 * j, _dot(xb_ref[...], w1_ref[0]))
        raw_ref[...] = _dot(xb_ref[...], w1_ref[1])

    @pl.when(j == 0)
    def _first_step():
        x_copy(i).wait()
        for r in range(tm // RB_A):
            rows = slice(r * RB_A, (r + 1) * RB_A)
            x = stage_ref[rows, :]
            xb_ref[rows, :] = x.astype(BF16)
            o_ref[rows, :] = alpha * x
        sum_ref[...] = jnp.zeros_like(sum_ref)
        sq_ref[...] = jnp.zeros_like(sq_ref)
        phase_v_body(first=True)

    @pl.when((j == 1) & (i + 1 < n_tiles))
    def _prefetch():
        x_copy(i + 1).start()

    @pl.when((j > 0) & (j < npv))
    def _phase_v():
        phase_v_body(first=False)

    def finish_stats():
        gelu_and_stats(nc - 1, raw_ref[...])
        mu = jnp.sum(sum_ref[...], axis=1, keepdims=True) * (1.0 / d_inner)
        var = jnp.sum(sq_ref[...], axis=1, keepdims=True) * (1.0 / d_inner) - mu * mu
        mu_ref[...] = mu
        rstd_ref[...] = lax.rsqrt(var + LN_EPS)

    def mix_body(first=False):
        c = j - npv
        if first:
            finish_stats()
        xb = xb_ref[...]
        gate = _gelu_tanh(_dot(xb, w1_ref[0])) * _silu(_dot(xb, w1_ref[1]))
        vn = ((gv_ref[c].astype(F32) - mu_ref[...]) * rstd_ref[...] * vg_ref[...]
              + vb_ref[...]).astype(BF16)
        for g in range(cw // gdim):
            cols = slice(g * gdim, (g + 1) * gdim)
            wsg = ws_ref[g]
            bsg = jnp.broadcast_to(bs_ref[g], (CHUNK_A, gdim))
            for n in range(tm // CHUNK_A):
                rows = slice(n * CHUNK_A, (n + 1) * CHUNK_A)
                s = _dot(wsg, vn[rows, cols]) + bsg
                y_ref[rows, cols] = (gate[rows, cols] * s).astype(BF16)
        o_ref[...] += _dot(y_ref[...], wo_ref[...])

    @pl.when(j == npv)
    def _first_mix():
        mix_body(first=True)

    @pl.when(j > npv)
    def _mix():
        mix_body()

    @pl.when(j == npv + nc - 1)
    def _finish():
        for r in range(tm // RB_A):
            rows = slice(r * RB_A, (r + 1) * RB_A)
            o_ref[rows, :] = _layer_norm_rows(o_ref[rows, :], lng_ref[...], lnb_ref[...])


def _layer_a(x2, p, lng, lnb, alpha):
    rows, d = x2.shape
    w1, w_out = p["w1"], p["w_out"]
    d_inner = w_out.shape[0]
    cw = w1.shape[3]
    nc = d_inner // cw
    npv = nc // 2
    gpc = cw // (d_inner // N_GROUPS_A)
    tm = min(TM_A, rows)
    grid = (rows // tm, npv + nc)

    def cmix(j):
        return jnp.maximum(j - npv, 0)

    kernel = functools.partial(_layer_a_kernel, nc=nc, d_inner=d_inner, alpha=alpha, n_tiles=grid[0])
    return pl.pallas_call(
        kernel,
        grid=grid,
        in_specs=[
            pl.BlockSpec(memory_space=pl.ANY),
            pl.BlockSpec((None, 2, d, cw), lambda i, j: (j, 0, 0, 0)),
            pl.BlockSpec((cw, d), lambda i, j: (cmix(j), 0)),
            pl.BlockSpec((gpc, CHUNK_A, CHUNK_A), lambda i, j: (cmix(j), 0, 0)),
            pl.BlockSpec((gpc, CHUNK_A, 1), lambda i, j: (cmix(j), 0, 0)),
            pl.BlockSpec((1, cw), lambda i, j: (0, cmix(j))),
            pl.BlockSpec((1, cw), lambda i, j: (0, cmix(j))),
            pl.BlockSpec((1, d), lambda i, j: (0, 0)),
            pl.BlockSpec((1, d), lambda i, j: (0, 0)),
        ],
        out_specs=pl.BlockSpec((tm, d), lambda i, j: (i, 0)),
        out_shape=jax.ShapeDtypeStruct((rows, d), F32),
        scratch_shapes=[
            pltpu.VMEM((tm, d), F32),
            pltpu.VMEM((tm, d), BF16),
            pltpu.VMEM((nc, tm, cw), BF16),
            pltpu.VMEM((tm, cw), F32),
            pltpu.VMEM((tm, 128), F32),
            pltpu.VMEM((tm, 128), F32),
            pltpu.VMEM((tm, 1), F32),
            pltpu.VMEM((tm, 1), F32),
            pltpu.VMEM((tm, cw), BF16),
            pltpu.SemaphoreType.DMA,
        ],
        compiler_params=pltpu.CompilerParams(
            dimension_semantics=("arbitrary", "arbitrary"),
            vmem_limit_bytes=VMEM_LIMIT),
        name="layer_a",
    )(x2, w1, w_out, p["w_s"], p["b_s"], p["vg"], p["vb"], lng, lnb)


def _proj_b_kernel(x_ref, w_ref, wgl_ref, wg_ref, bg_ref, lf_ref, lb_ref,
                   h_ref, gf_ref, rb_ref, xb_ref, gl_ref, *, key_dim, z_chunk0):
    j = pl.program_id(1)
    tm = x_ref.shape[0]

    def project(gate_act=False):
        res = _dot(xb_ref[...], w_ref[...])
        for c in range(h_ref.shape[0]):
            blk = res[:, c * COL_B:(c + 1) * COL_B]
            h_ref[c] = (_silu(blk) if gate_act else blk).astype(BF16)

    def gates(r):
        rows = slice(r * CUM_ROWS, (r + 1) * CUM_ROWS)
        pre = _dot(gl_ref[rows, :], wg_ref[...]) + bg_ref[...]
        glog = _log_sigmoid(pre) * (LOG2_E / GATE_TAU)
        for dst, tri, cols in ((gf_ref, lf_ref, slice(0, key_dim)),
                               (rb_ref, lb_ref, slice(key_dim, 2 * key_dim))):
            gb = glog[:, cols]
            hi = gb.astype(BF16)
            lo = (gb - hi.astype(F32)).astype(BF16)
            cs = _dot(tri[...], hi) + _dot(tri[...], lo)
            for c in range(key_dim // COL_B):
                dst[c, rows, :] = cs[:, c * COL_B:(c + 1) * COL_B]

    @pl.when(j == 0)
    def _gates_and_first_chunk():
        for r in range(tm // CUM_ROWS):
            rows = slice(r * CUM_ROWS, (r + 1) * CUM_ROWS)
            xb_ref[rows, :] = x_ref[rows, :].astype(BF16)
        gl_t = _dot_nt(wgl_ref[...], xb_ref[...])
        gl_ref[...] = gl_t.T.astype(BF16)
        for r in range(tm // CUM_ROWS):
            gates(r)
        project()

    @pl.when((j > 0) & (j < z_chunk0))
    def _qkv_chunks():
        project()

    @pl.when(j >= z_chunk0)
    def _z_chunks():
        project(gate_act=True)


def _proj_b(x2, p):
    rows, d = x2.shape
    w_main = p["w_main"]
    n_chunks, _, cw = w_main.shape
    key_dim = p["key_dim"]
    tm = min(TM_P, rows)
    grid = (rows // tm, n_chunks)
    z_chunk0 = (2 * key_dim + p["d_inner"]) // cw
    assert z_chunk0 * cw == 2 * key_dim + p["d_inner"] and z_chunk0 > 0
    kernel = functools.partial(_proj_b_kernel, key_dim=key_dim, z_chunk0=z_chunk0)
    small = [p["w_gl"], p["w_gate"], p["b_gate"], p["tri_f"], p["tri_b"]]
    return pl.pallas_call(
        kernel,
        grid=grid,
        in_specs=[
            pl.BlockSpec((tm, d), lambda i, j: (i, 0)),
            pl.BlockSpec((None, d, cw), lambda i, j: (j, 0, 0)),
        ] + [pl.BlockSpec(a.shape, lambda i, j: (0, 0)) for a in small],
        out_specs=[
            pl.BlockSpec((cw // COL_B, tm, COL_B), lambda i, j: (j, i, 0)),
            pl.BlockSpec((key_dim // COL_B, tm, COL_B), lambda i, j: (0, i, 0)),
            pl.BlockSpec((key_dim // COL_B, tm, COL_B), lambda i, j: (0, i, 0)),
        ],
        out_shape=[
            jax.ShapeDtypeStruct((n_chunks * cw // COL_B, rows, COL_B), BF16),
            jax.ShapeDtypeStruct((key_dim // COL_B, rows, COL_B), F32),
            jax.ShapeDtypeStruct((key_dim // COL_B, rows, COL_B), F32),
        ],
        scratch_shapes=[pltpu.VMEM((tm, d), BF16),
                        pltpu.VMEM((tm, p["w_gl"].shape[0]), BF16)],
        compiler_params=pltpu.CompilerParams(
            dimension_semantics=("arbitrary", "arbitrary"),
            vmem_limit_bytes=VMEM_LIMIT),
        name="proj_b",
    )(x2, w_main, *small)


def _row_to_col(vec_row, width):
    n = vec_row.shape[1]
    return jnp.broadcast_to(vec_row, (width, n)).T


def _gla_macro_chunk(q, k, v_blocks, cum, state_blocks, reverse):
    c_rows, dk = q.shape
    ns = c_rows // SUB_B
    qs = q.astype(F32) * (dk ** -0.5)
    kf = k.astype(F32)
    order = list(range(ns))[::-1] if reverse else list(range(ns))
    mid_row = SUB_B // 2 - 1 if reverse else SUB_B // 2
    end_row = 0 if reverse else SUB_B - 1

    rsl = {a: slice(a * SUB_B, (a + 1) * SUB_B) for a in range(ns)}
    r_end, q_diag, k_diag, q_off, k_off = {}, {}, {}, {}, {}
    r_prev = {}
    zero_row = jnp.zeros((1, dk), F32)
    prev = zero_row
    for a in order:
        ca = cum[rsl[a]]
        mid = ca[mid_row:mid_row + 1]
        r_end[a] = ca[end_row:end_row + 1]
        r_prev[a] = prev
        q_diag[a] = (qs[rsl[a]] * jnp.exp2(ca - mid)).astype(BF16)
        k_diag[a] = (kf[rsl[a]] * jnp.exp2(mid - ca)).astype(BF16)
        q_off[a] = qs[rsl[a]] * jnp.exp2(ca - prev)
        k_off[a] = kf[rsl[a]] * jnp.exp2(r_end[a] - ca)
        prev = r_end[a]
    r_last = prev

    ri = lax.broadcasted_iota(jnp.int32, (SUB_B, SUB_B), 0)
    ci = lax.broadcasted_iota(jnp.int32, (SUB_B, SUB_B), 1)
    keep = (ci > ri) if reverse else (ci <= ri)

    k_off_b = {a: k_off[a].astype(BF16) for a in range(ns)}
    a_rows = []
    q_state = []
    for a in range(ns):
        blocks = []
        for b in range(ns):
            if b == a:
                blk = jnp.where(keep, _dot_nt(q_diag[a], k_diag[a]), 0.0)
            elif order.index(b) < order.index(a):
                if order.index(b) + 1 == order.index(a):
                    qa = q_off[a]
                else:
                    qa = q_off[a] * jnp.exp2(r_prev[a] - r_end[b])
                blk = _dot_nt(qa.astype(BF16), k_off_b[b])
            else:
                blk = jnp.zeros((SUB_B, SUB_B), F32)
            blocks.append(blk)
        a_rows.append(jnp.concatenate(blocks, axis=1) if ns > 1 else blocks[0])
        q_state.append(q_off[a] * jnp.exp2(r_prev[a]))
    a_full = jnp.concatenate(a_rows, axis=0) if ns > 1 else a_rows[0]
    q_st = jnp.concatenate(q_state, axis=0) if ns > 1 else q_state[0]

    a_b = a_full.astype(BF16)
    q_b = q_st.astype(BF16)
    k_dec = [k_off[a] * jnp.exp2(r_last - r_end[a]) for a in range(ns)]
    k_dec = (jnp.concatenate(k_dec, axis=0) if ns > 1 else k_dec[0]).astype(BF16)
    dcol = _row_to_col(jnp.exp2(r_last), 128)
    dcol = jnp.concatenate([dcol] * (v_blocks[0].shape[1] // 128), axis=1)
    o_blocks, new_state = [], []
    for vc, sc in zip(v_blocks, state_blocks):
        o_blocks.append(_dot(a_b, vc) + _dot(q_b, sc.astype(BF16)))
        new_state.append(sc * dcol + _dot_tn(k_dec, vc))
    return o_blocks, new_state


def _gla_sweep_tile(q_ref, k_ref, v_ref, cum_ref, s_ref, h, emit, reverse):
    tb = q_ref.shape[0]
    nvb = v_ref.shape[0]
    nm = tb // MACRO_B
    chunks = range(nm - 1, -1, -1) if reverse else range(nm)
    for m in chunks:
        rows = slice(m * MACRO_B, (m + 1) * MACRO_B)
        o_blocks, s_new = _gla_macro_chunk(
            q_ref[rows, :], k_ref[rows, :], [v_ref[c, rows, :] for c in range(nvb)], cum_ref[rows, :],
            [s_ref[h, c] for c in range(nvb)], reverse)
        for c in range(nvb):
            s_ref[h, c] = s_new[c]
        emit(rows, o_blocks)


def _gla_fwd_kernel(q_ref, k_ref, v_ref, cum_ref, o_ref, s_ref):
    i = pl.program_id(1)
    h = pl.program_id(2)

    @pl.when(i == 0)
    def _reset():
        s_ref[h] = jnp.zeros(s_ref.shape[1:], F32)

    def emit(rows, o_blocks):
        for c, o in enumerate(o_blocks):
            o_ref[c, rows, :] = o.astype(o_ref.dtype)

    _gla_sweep_tile(q_ref, k_ref, v_ref, cum_ref, s_ref, h, emit, reverse=False)


def _gla_dims(hmain, batch, key_dim, d_inner, tile):
    rows = hmain.shape[1]
    dk = key_dim // GLA_HEADS
    dv = d_inner // GLA_HEADS
    assert dk == COL_B and dv % COL_B == 0
    tb = min(tile, rows // batch)
    nt = rows // batch // tb
    nvb = dv // COL_B
    kk = key_dim // COL_B
    kv = 2 * key_dim // (nvb * COL_B)
    kz = (2 * key_dim + d_inner) // (nvb * COL_B)
    return rows, dk, dv, tb, nt, nvb, kk, kv, kz


def _gla_fwd(hmain, cum_f, batch, key_dim, d_inner):
    rows, dk, dv, tb, nt, nvb, kk, kv, _ = _gla_dims(hmain, batch, key_dim, d_inner, TB_GF)
    grid = (batch, nt, GLA_HEADS)
    return pl.pallas_call(
        _gla_fwd_kernel,
        grid=grid,
        in_specs=[
            pl.BlockSpec((None, tb, dk), lambda b, i, h: (h, b * nt + i, 0)),
            pl.BlockSpec((None, tb, dk), lambda b, i, h: (kk + h, b * nt + i, 0)),
            pl.BlockSpec((nvb, tb, COL_B), lambda b, i, h: (kv + h, b * nt + i, 0)),
            pl.BlockSpec((None, tb, dk), lambda b, i, h: (h, b * nt + i, 0)),
        ],
        out_specs=pl.BlockSpec((nvb, tb, COL_B), lambda b, i, h: (h, b * nt + i, 0)),
        out_shape=jax.ShapeDtypeStruct((GLA_HEADS * nvb, rows, COL_B), BF16),
        scratch_shapes=[pltpu.VMEM((GLA_HEADS, nvb, dk, COL_B), F32)],
        compiler_params=pltpu.CompilerParams(
            dimension_semantics=("arbitrary", "arbitrary", "arbitrary"),
            vmem_limit_bytes=VMEM_LIMIT),
        name="gla_fwd",
    )(hmain, hmain, hmain, cum_f)


def _gla_bwd_kernel(q_ref, k_ref, v_ref, cum_ref, of_ref, z_ref, x_ref, wo_ref, gn_ref,
                    lng_ref, lnb_ref, o_ref, s_ref, y_ref, *, alpha):
    i = pl.program_id(1)
    h = pl.program_id(2)
    tb = o_ref.shape[0]
    xr = x_ref.shape[0]

    @pl.when(i == 0)
    def _reset():
        s_ref[h] = jnp.zeros(s_ref.shape[1:], F32)

    @pl.when(h == 0)
    def _zero_acc():
        o_ref[...] = jnp.zeros_like(o_ref)

    def emit(rows, o_blocks):
        nvb = len(o_blocks)
        o = [of_ref[c, rows, :].astype(F32) + o_blocks[c] for c in range(nvb)]
        ss = sum(jnp.sum(oc * oc, axis=-1, keepdims=True) for oc in o)
        rstd = lax.rsqrt(ss * (1.0 / (nvb * COL_B)) + RMS_EPS)
        for c in range(nvb):
            cols = slice(c * COL_B, (c + 1) * COL_B)
            y_ref[rows, cols] = (o[c] * rstd * gn_ref[:, cols]).astype(BF16) * z_ref[c, rows, :]

    _gla_sweep_tile(q_ref, k_ref, v_ref, cum_ref, s_ref, h, emit, reverse=True)

    o_ref[...] += _dot(y_ref[...], wo_ref[...])
    xrows = pl.ds(pl.multiple_of(h * xr, xr), xr)
    o_ref[xrows, :] += alpha * x_ref[...]

    @pl.when(h == GLA_HEADS - 1)
    def _finish():
        for r in range(tb // CUM_ROWS):
            rows = slice(r * CUM_ROWS, (r + 1) * CUM_ROWS)
            o_ref[rows, :] = _layer_norm_rows(o_ref[rows, :], lng_ref[...], lnb_ref[...])


def _gla_bwd(hmain, cum_b, o_f, x2, p, lng, lnb, batch, alpha):
    d = x2.shape[1]
    w_out = p["w_out"]
    key_dim = p["key_dim"]
    d_inner = w_out.shape[0]
    rows, dk, dv, tb, nt, nvb, kk, kv, kz = _gla_dims(hmain, batch, key_dim, d_inner, TB_GB)
    grid = (batch, nt, GLA_HEADS)

    def row(b, i):
        return b * nt + (nt - 1 - i)

    xr = tb // GLA_HEADS
    kernel = functools.partial(_gla_bwd_kernel, alpha=alpha)
    return pl.pallas_call(
        kernel,
        grid=grid,
        in_specs=[
            pl.BlockSpec((None, tb, dk), lambda b, i, h: (h, row(b, i), 0)),
            pl.BlockSpec((None, tb, dk), lambda b, i, h: (kk + h, row(b, i), 0)),
            pl.BlockSpec((nvb, tb, COL_B), lambda b, i, h: (kv + h, row(b, i), 0)),
            pl.BlockSpec((None, tb, dk), lambda b, i, h: (h, row(b, i), 0)),
            pl.BlockSpec((nvb, tb, COL_B), lambda b, i, h: (h, row(b, i), 0)),
            pl.BlockSpec((nvb, tb, COL_B), lambda b, i, h: (kz + h, row(b, i), 0)),
            pl.BlockSpec((xr, d), lambda b, i, h: (row(b, i) * GLA_HEADS + h, 0)),
            pl.BlockSpec((dv, d), lambda b, i, h: (h, 0)),
            pl.BlockSpec((1, dv), lambda b, i, h: (0, h)),
            pl.BlockSpec((1, d), lambda b, i, h: (0, 0)),
            pl.BlockSpec((1, d), lambda b, i, h: (0, 0)),
        ],
        out_specs=pl.BlockSpec((tb, d), lambda b, i, h: (row(b, i), 0)),
        out_shape=jax.ShapeDtypeStruct((rows, d), F32),
        scratch_shapes=[
            pltpu.VMEM((GLA_HEADS, nvb, dk, COL_B), F32),
            pltpu.VMEM((tb, dv), BF16),
        ],
        compiler_params=pltpu.CompilerParams(
            dimension_semantics=("arbitrary", "arbitrary", "arbitrary"),
            vmem_limit_bytes=VMEM_LIMIT),
        name="gla_bwd",
    )(hmain, hmain, hmain, cum_b, o_f, hmain, x2, w_out, p["gn"], lng, lnb)


def _tri_blocks(reverse):
    r = jnp.arange(CUM_ROWS)[:, None]
    c = jnp.arange(CUM_ROWS)[None, :]
    same = (r // MACRO_B) == (c // MACRO_B)
    tri = (c >= r) if reverse else (c <= r)
    return (same & tri).astype(BF16)


def _cast_block_kernel(w_ref, o_ref):
    o_ref[...] = w_ref[...].astype(BF16)


def _relayout_cols(w, cw, n_slots, src_chunk):
    k = w.shape[0]
    return pl.pallas_call(
        _cast_block_kernel,
        grid=(n_slots,),
        in_specs=[pl.BlockSpec((k, cw), lambda s: (0, src_chunk(s)))],
        out_specs=pl.BlockSpec((None, k, cw), lambda s: (s, 0, 0)),
        out_shape=jax.ShapeDtypeStruct((n_slots, k, cw), BF16),
        compiler_params=pltpu.CompilerParams(dimension_semantics=("arbitrary",)),
        name="weight_relayout",
    )(w)


def _prep_a(w_in3, layer, vg, vb, w_s, b_s, w_out):
    d = w_in3.shape[1]
    d_inner = w_out.shape[0]
    nc = d_inner // CW_A
    npv = nc // 2

    def src_chunk(s):
        pair, t = s // 2, s % 2
        return jnp.where(pair < npv, nc + s, jnp.where(t == 0, pair - npv, 2 * nc + pair - npv))

    w1 = _relayout_cols(w_in3[layer], CW_A, 3 * nc, src_chunk).reshape(npv + nc, 2, d, CW_A)
    return dict(w1=w1, w_s=w_s.astype(BF16),
                b_s=b_s[:, :, None].astype(F32), w_out=w_out.astype(BF16),
                vg=vg[None, :], vb=vb[None, :])


def _prep_b(w_in3, layer, w_g2, b_g, gn_g, w_out):
    key_dim = w_g2.shape[2]
    d_inner = w_out.shape[0]
    n_main = 2 * key_dim + 2 * d_inner
    zeros = jnp.zeros((GATE_RANK, key_dim), F32)
    w_gate = jnp.concatenate([jnp.concatenate([w_g2[0], zeros], axis=1),
                              jnp.concatenate([zeros, w_g2[1]], axis=1)], axis=0)
    return dict(w_main=_relayout_cols(w_in3[layer], CW_P, n_main // CW_P, lambda s: s),
                w_gl=w_in3[layer, :, n_main:].T.astype(BF16),
                w_gate=w_gate.astype(BF16), b_gate=b_g.reshape(1, 2 * key_dim),
                gn=gn_g[None, :], w_out=w_out.astype(BF16), key_dim=key_dim, d_inner=d_inner,
                tri_f=_tri_blocks(False), tri_b=_tri_blocks(True))


def _mixer_b_layer(x2, batch, p, lng, lnb, alpha):
    hmain, cum_f, cum_b = _proj_b(x2, p)
    o_f = _gla_fwd(hmain, cum_f, batch, p["key_dim"], p["d_inner"])
    return _gla_bwd(hmain, cum_b, o_f, x2, p, lng, lnb, batch, alpha)


def _trunk(x, layers, ln_g, ln_b, alpha):
    batch, seq, d = x.shape
    x2 = x.reshape(batch * seq, d)
    for i, (kind, p) in enumerate(layers):
        lng, lnb = ln_g[i][None, :], ln_b[i][None, :]
        if kind == "a":
            x2 = _layer_a(x2, p, lng, lnb, alpha)
        else:
            x2 = _mixer_b_layer(x2, batch, p, lng, lnb, alpha)
    return x2.reshape(batch, seq, d)


def kernel(x_prompt, x_sample, w_in_a, ln_v_g_a, ln_v_b_a, w_s_a, b_s_a, w_out_a, w_in_b, w_g2_b, b_g_b, gn_g_b, w_out_b, ln_g, ln_b):
    depth = ln_g.shape[0]
    alpha = (2 * depth) ** 0.25
    layers = []
    for i in range(depth):
        j = i // 2
        if i % 2 == 0:
            layers.append(("a", _prep_a(w_in_a, j, ln_v_g_a[j], ln_v_b_a[j], w_s_a[j], b_s_a[j], w_out_a[j])))
        else:
            layers.append(("b", _prep_b(w_in_b, j, w_g2_b[j], b_g_b[j], gn_g_b[j], w_out_b[j])))
    y_prompt = _trunk(x_prompt, layers, ln_g, ln_b, alpha)
    y_sample = _trunk(x_sample, layers, ln_g, ln_b, alpha)
    return (y_prompt, y_sample)
```

```python
import functools

import jax
import jax.numpy as jnp
from jax import lax
from jax.experimental import pallas as pl
from jax.experimental.pallas import tpu as pltpu

F32 = jnp.float32
BF16 = jnp.bfloat16

CHUNK_A = 128
N_GROUPS_A = 16
GLA_HEADS = 4
GATE_RANK = 16
GATE_TAU = 16.0
LOG2_E = 1.4426950408889634
SUB_B = 64
LN_EPS = 1e-5
RMS_EPS = 1e-6

TM_A = 1024
CW_A = 512
RB_A = 256
MXU_COLS = 256
TM_P = 1024
CW_P = 1024
TB_GF = 2048
TB_GB = 1024
MACRO_B = 256
CUM_ROWS = 256
COL_B = 256
VMEM_LIMIT = 60 * 1024 * 1024


def _dot(a, b):
    return jnp.dot(a, b, preferred_element_type=F32)


def _dot_nt(a, b):
    return lax.dot_general(a, b, (((1,), (1,)), ((), ())), preferred_element_type=F32)


def _dot_tn(a, b):
    return lax.dot_general(a, b, (((0,), (0,)), ((), ())), preferred_element_type=F32)


def _gelu_tanh(x):
    c = 0.7978845608028654
    hx = 0.5 * x
    return hx + hx * jnp.tanh(x * (c + (c * 0.044715) * (x * x)))


def _silu(x):
    hx = 0.5 * x
    return hx + hx * jnp.tanh(hx)


def _log_sigmoid(x):
    return jnp.minimum(x, 0.0) - jnp.log(1.0 + jnp.exp(-jnp.abs(x)))


def _layer_norm_rows(h, g, b):
    mu = jnp.mean(h, axis=-1, keepdims=True)
    hc = h - mu
    var = jnp.mean(hc * hc, axis=-1, keepdims=True)
    return hc * lax.rsqrt(var + LN_EPS) * g + b


def _layer_a_kernel(x_hbm, w1_ref, wo_ref, ws_ref, bs_ref, vg_ref, vb_ref,
                    lng_ref, lnb_ref, o_ref,
                    stage_ref, xb_ref, gv_ref, raw_ref, sum_ref, sq_ref, mu_ref, rstd_ref, y_ref, sem,
                    *, nc, d_inner, alpha, n_tiles):
    i = pl.program_id(0)
    j = pl.program_id(1)
    tm = o_ref.shape[0]
    cw = w1_ref.shape[2]
    gdim = d_inner // N_GROUPS_A
    npv = nc // 2
    lanes = sum_ref.shape[1]

    def x_copy(tile):
        return pltpu.make_async_copy(x_hbm.at[pl.ds(tile * tm, tm), :], stage_ref, sem)

    @pl.when((j == 0) & (i == 0))
    def _first_fetch():
        x_copy(0).start()

    def gelu_and_stats(chunk, v):
        for pc in range(cw // MXU_COLS):
            cols = slice(pc * MXU_COLS, (pc + 1) * MXU_COLS)
            gvb = _gelu_tanh(v[:, cols]).astype(BF16)
            gv_ref[chunk, :, cols] = gvb
            gv = gvb.astype(F32)
            part = gv[:, 0:lanes]
            part2 = part * part
            for l in range(1, MXU_COLS // lanes):
                blk = gv[:, l * lanes:(l + 1) * lanes]
                part = part + blk
                part2 = part2 + blk * blk
            sum_ref[...] += part
            sq_ref[...] += part2

    def phase_v_body(first):
        if not first:
            gelu_and_stats(2 * j - 1, raw_ref[...])
        gelu_and_stats(2 * j, _dot(xb_ref[...], w1_ref[0]))
        raw_ref[...] = _dot(xb_ref[...], w1_ref[1])

    @pl.when(j == 0)
    def _first_step():
        x_copy(i).wait()
        for r in range(tm // RB_A):
            rows = slice(r * RB_A, (r + 1) * RB_A)
            x = stage_ref[rows, :]
            xb_ref[rows, :] = x.astype(BF16)
            o_ref[rows, :] = alpha * x
        sum_ref[...] = jnp.zeros_like(sum_ref)
        sq_ref[...] = jnp.zeros_like(sq_ref)
        phase_v_body(first=True)

    @pl.when((j == 1) & (i + 1 < n_tiles))
    def _prefetch():
        x_copy(i + 1).start()

    @pl.when((j > 0) & (j < npv))
    def _phase_v():
        phase_v_body(first=False)

    def finish_stats():
        gelu_and_stats(nc - 1, raw_ref[...])
        mu = jnp.sum(sum_ref[...], axis=1, keepdims=True) * (1.0 / d_inner)
        var = jnp.sum(sq_ref[...], axis=1, keepdims=True) * (1.0 / d_inner) - mu * mu
        var = jnp.maximum(var, 0.0)
        mu_ref[...] = mu
        rstd_ref[...] = lax.rsqrt(var + LN_EPS)

    def mix_body(first=False):
        c = j - npv
        if first:
            finish_stats()
        xb = xb_ref[...]
        gate = _gelu_tanh(_dot(xb, w1_ref[0])) * _silu(_dot(xb, w1_ref[1]))
        vn = ((gv_ref[c].astype(F32) - mu_ref[...]) * rstd_ref[...] * vg_ref[...]
              + vb_ref[...]).astype(BF16)
        for g in range(cw // gdim):
            cols = slice(g * gdim, (g + 1) * gdim)
            wsg = ws_ref[g]
            bsg = jnp.broadcast_to(bs_ref[g], (CHUNK_A, gdim))
            for n in range(tm // CHUNK_A):
                rows = slice(n * CHUNK_A, (n + 1) * CHUNK_A)
                s = _dot(wsg, vn[rows, cols]) + bsg
                y_ref[rows, cols] = (gate[rows, cols] * s).astype(BF16)
        o_ref[...] += _dot(y_ref[...], wo_ref[...])

    @pl.when(j == npv)
    def _first_mix():
        mix_body(first=True)

    @pl.when(j > npv)
    def _mix():
        mix_body()

    @pl.when(j == npv + nc - 1)
    def _finish():
        for r in range(tm // RB_A):
            rows = slice(r * RB_A, (r + 1) * RB_A)
            o_ref[rows, :] = _layer_norm_rows(o_ref[rows, :], lng_ref[...], lnb_ref[...])


def _layer_a(x2, p, lng, lnb, alpha):
    rows, d = x2.shape
    w1, w_out = p["w1"], p["w_out"]
    d_inner = w_out.shape[0]
    cw = w1.shape[3]
    nc = d_inner // cw
    npv = nc // 2
    gpc = cw // (d_inner // N_GROUPS_A)
    tm = min(TM_A, rows)
    grid = (rows // tm, npv + nc)

    def cmix(j):
        return jnp.maximum(j - npv, 0)

    kernel = functools.partial(_layer_a_kernel, nc=nc, d_inner=d_inner, alpha=alpha, n_tiles=grid[0])
    return pl.pallas_call(
        kernel,
        grid=grid,
        in_specs=[
            pl.BlockSpec(memory_space=pl.ANY),
            pl.BlockSpec((None, 2, d, cw), lambda i, j: (j, 0, 0, 0)),
            pl.BlockSpec((cw, d), lambda i, j: (cmix(j), 0)),
            pl.BlockSpec((gpc, CHUNK_A, CHUNK_A), lambda i, j: (cmix(j), 0, 0)),
            pl.BlockSpec((gpc, CHUNK_A, 1), lambda i, j: (cmix(j), 0, 0)),
            pl.BlockSpec((1, cw), lambda i, j: (0, cmix(j))),
            pl.BlockSpec((1, cw), lambda i, j: (0, cmix(j))),
            pl.BlockSpec((1, d), lambda i, j: (0, 0)),
            pl.BlockSpec((1, d), lambda i, j: (0, 0)),
        ],
        out_specs=pl.BlockSpec((tm, d), lambda i, j: (i, 0)),
        out_shape=jax.ShapeDtypeStruct((rows, d), F32),
        scratch_shapes=[
            pltpu.VMEM((tm, d), F32),
            pltpu.VMEM((tm, d), BF16),
            pltpu.VMEM((nc, tm, cw), BF16),
            pltpu.VMEM((tm, cw), F32),
            pltpu.VMEM((tm, 128), F32),
            pltpu.VMEM((tm, 128), F32),
            pltpu.VMEM((tm, 1), F32),
            pltpu.VMEM((tm, 1), F32),
            pltpu.VMEM((tm, cw), BF16),
            pltpu.SemaphoreType.DMA,
        ],
        compiler_params=pltpu.CompilerParams(
            dimension_semantics=("arbitrary", "arbitrary"),
            vmem_limit_bytes=VMEM_LIMIT),
        name="layer_a",
    )(x2, w1, w_out, p["w_s"], p["b_s"], p["vg"], p["vb"], lng, lnb)


def _proj_b_kernel(x_ref, w_ref, wgl_ref, wg_ref, bg_ref, lf_ref, lb_ref,
                   h_ref, gf_ref, rb_ref, xb_ref, gl_ref, *, key_dim, z_chunk0):
    j = pl.program_id(1)
    tm = x_ref.shape[0]

    def project(gate_act=False):
        res = _dot(xb_ref[...], w_ref[...])
        for c in range(h_ref.shape[0]):
            blk = res[:, c * COL_B:(c + 1) * COL_B]
            h_ref[c] = (_silu(blk) if gate_act else blk).astype(BF16)

    def gates(r):
        rows = slice(r * CUM_ROWS, (r + 1) * CUM_ROWS)
        pre = _dot(gl_ref[rows, :], wg_ref[...]) + bg_ref[...]
        glog = _log_sigmoid(pre) * (LOG2_E / GATE_TAU)
        for dst, tri, cols in ((gf_ref, lf_ref, slice(0, key_dim)),
                               (rb_ref, lb_ref, slice(key_dim, 2 * key_dim))):
            gb = glog[:, cols]
            hi = gb.astype(BF16)
            lo = (gb - hi.astype(F32)).astype(BF16)
            cs = _dot(tri[...], hi) + _dot(tri[...], lo)
            for c in range(key_dim // COL_B):
                dst[c, rows, :] = cs[:, c * COL_B:(c + 1) * COL_B]

    @pl.when(j == 0)
    def _gates_and_first_chunk():
        for r in range(tm // CUM_ROWS):
            rows = slice(r * CUM_ROWS, (r + 1) * CUM_ROWS)
            xb_ref[rows, :] = x_ref[rows, :].astype(BF16)
        gl_t = _dot_nt(wgl_ref[...], xb_ref[...])
        gl_ref[...] = gl_t.T.astype(BF16)
        for r in range(tm // CUM_ROWS):
            gates(r)
        project()

    @pl.when((j > 0) & (j < z_chunk0))
    def _qkv_chunks():
        project()

    @pl.when(j >= z_chunk0)
    def _z_chunks():
        project(gate_act=True)


def _proj_b(x2, p):
    rows, d = x2.shape
    w_main = p["w_main"]
    n_chunks, _, cw = w_main.shape
    key_dim = p["key_dim"]
    tm = min(TM_P, rows)
    grid = (rows // tm, n_chunks)
    z_chunk0 = (2 * key_dim + p["d_inner"]) // cw
    assert z_chunk0 * cw == 2 * key_dim + p["d_inner"] and z_chunk0 > 0
    kernel = functools.partial(_proj_b_kernel, key_dim=key_dim, z_chunk0=z_chunk0)
    small = [p["w_gl"], p["w_gate"], p["b_gate"], p["tri_f"], p["tri_b"]]
    return pl.pallas_call(
        kernel,
        grid=grid,
        in_specs=[
            pl.BlockSpec((tm, d), lambda i, j: (i, 0)),
            pl.BlockSpec((None, d, cw), lambda i, j: (j, 0, 0)),
        ] + [pl.BlockSpec(a.shape, lambda i, j: (0, 0)) for a in small],
        out_specs=[
            pl.BlockSpec((cw // COL_B, tm, COL_B), lambda i, j: (j, i, 0)),
            pl.BlockSpec((key_dim // COL_B, tm, COL_B), lambda i, j: (0, i, 0)),
            pl.BlockSpec((key_dim // COL_B, tm, COL_B), lambda i, j: (0, i, 0)),
        ],
        out_shape=[
            jax.ShapeDtypeStruct((n_chunks * cw // COL_B, rows, COL_B), BF16),
            jax.ShapeDtypeStruct((key_dim // COL_B, rows, COL_B), F32),
            jax.ShapeDtypeStruct((key_dim // COL_B, rows, COL_B), F32),
        ],
        scratch_shapes=[pltpu.VMEM((tm, d), BF16),
                        pltpu.VMEM((tm, p["w_gl"].shape[0]), BF16)],
        compiler_params=pltpu.CompilerParams(
            dimension_semantics=("arbitrary", "arbitrary"),
            vmem_limit_bytes=VMEM_LIMIT),
        name="proj_b",
    )(x2, w_main, *small)


def _row_to_col(vec_row, width):
    n = vec_row.shape[1]
    return jnp.broadcast_to(vec_row, (width, n)).T


def _gla_macro_chunk(q, k, v_blocks, cum, state_blocks, reverse):
    c_rows, dk = q.shape
    ns = c_rows // SUB_B
    qs = q.astype(F32) * (dk ** -0.5)
    kf = k.astype(F32)
    order = list(range(ns))[::-1] if reverse else list(range(ns))
    mid_row = SUB_B // 2 - 1 if reverse else SUB_B // 2
    end_row = 0 if reverse else SUB_B - 1

    rsl = {a: slice(a * SUB_B, (a + 1) * SUB_B) for a in range(ns)}
    r_end, q_diag, k_diag, q_off, k_off = {}, {}, {}, {}, {}
    r_prev = {}
    zero_row = jnp.zeros((1, dk), F32)
    prev = zero_row
    for a in order:
        ca = cum[rsl[a]]
        mid = ca[mid_row:mid_row + 1]
        r_end[a] = ca[end_row:end_row + 1]
        r_prev[a] = prev
        q_diag[a] = (qs[rsl[a]] * jnp.exp2(ca - mid)).astype(BF16)
        k_diag[a] = (kf[rsl[a]] * jnp.exp2(mid - ca)).astype(BF16)
        q_off[a] = qs[rsl[a]] * jnp.exp2(ca - prev)
        k_off[a] = kf[rsl[a]] * jnp.exp2(r_end[a] - ca)
        prev = r_end[a]
    r_last = prev

    ri = lax.broadcasted_iota(jnp.int32, (SUB_B, SUB_B), 0)
    ci = lax.broadcasted_iota(jnp.int32, (SUB_B, SUB_B), 1)
    keep = (ci > ri) if reverse else (ci <= ri)

    k_off_b = {a: k_off[a].astype(BF16) for a in range(ns)}
    a_rows = []
    q_state = []
    for a in range(ns):
        blocks = []
        for b in range(ns):
            if b == a:
                blk = jnp.where(keep, _dot_nt(q_diag[a], k_diag[a]), 0.0)
            elif order.index(b) < order.index(a):
                if order.index(b) + 1 == order.index(a):
                    qa = q_off[a]
                else:
                    qa = q_off[a] * jnp.exp2(r_prev[a] - r_end[b])
                blk = _dot_nt(qa.astype(BF16), k_off_b[b])
            else:
                blk = jnp.zeros((SUB_B, SUB_B), F32)
            blocks.append(blk)
        a_rows.append(jnp.concatenate(blocks, axis=1) if ns > 1 else blocks[0])
        q_state.append(q_off[a] * jnp.exp2(r_prev[a]))
    a_full = jnp.concatenate(a_rows, axis=0) if ns > 1 else a_rows[0]
    q_st = jnp.concatenate(q_state, axis=0) if ns > 1 else q_state[0]

    a_b = a_full.astype(BF16)
    q_b = q_st.astype(BF16)
    k_dec = [k_off[a] * jnp.exp2(r_last - r_end[a]) for a in range(ns)]
    k_dec = (jnp.concatenate(k_dec, axis=0) if ns > 1 else k_dec[0]).astype(BF16)
    dcol = _row_to_col(jnp.exp2(r_last), 128)
    dcol = jnp.concatenate([dcol] * (v_blocks[0].shape[1] // 128), axis=1)
    o_blocks, new_state = [], []
    for vc, sc in zip(v_blocks, state_blocks):
        o_blocks.append(_dot(a_b, vc) + _dot(q_b, sc.astype(BF16)))
        new_state.append(sc * dcol + _dot_tn(k_dec, vc))
    return o_blocks, new_state


def _gla_sweep_tile(q_ref, k_ref, v_ref, cum_ref, s_ref, h, emit, reverse):
    tb = q_ref.shape[0]
    nvb = v_ref.shape[0]
    nm = tb // MACRO_B
    chunks = range(nm - 1, -1, -1) if reverse else range(nm)
    for m in chunks:
        rows = slice(m * MACRO_B, (m + 1) * MACRO_B)
        o_blocks, s_new = _gla_macro_chunk(
            q_ref[rows, :], k_ref[rows, :], [v_ref[c, rows, :] for c in range(nvb)], cum_ref[rows, :],
            [s_ref[h, c] for c in range(nvb)], reverse)
        for c in range(nvb):
            s_ref[h, c] = s_new[c]
        emit(rows, o_blocks)


def _gla_fwd_kernel(q_ref, k_ref, v_ref, cum_ref, o_ref, s_ref):
    i = pl.program_id(1)
    h = pl.program_id(2)

    @pl.when(i == 0)
    def _reset():
        s_ref[h] = jnp.zeros(s_ref.shape[1:], F32)

    def emit(rows, o_blocks):
        for c, o in enumerate(o_blocks):
            o_ref[c, rows, :] = o.astype(o_ref.dtype)

    _gla_sweep_tile(q_ref, k_ref, v_ref, cum_ref, s_ref, h, emit, reverse=False)


def _gla_dims(hmain, batch, key_dim, d_inner, tile):
    rows = hmain.shape[1]
    dk = key_dim // GLA_HEADS
    dv = d_inner // GLA_HEADS
    assert dk == COL_B and dv % COL_B == 0
    tb = min(tile, rows // batch)
    nt = rows // batch // tb
    nvb = dv // COL_B
    kk = key_dim // COL_B
    kv = 2 * key_dim // (nvb * COL_B)
    kz = (2 * key_dim + d_inner) // (nvb * COL_B)
    return rows, dk, dv, tb, nt, nvb, kk, kv, kz


def _gla_fwd(hmain, cum_f, batch, key_dim, d_inner):
    rows, dk, dv, tb, nt, nvb, kk, kv, _ = _gla_dims(hmain, batch, key_dim, d_inner, TB_GF)
    grid = (batch, nt, GLA_HEADS)
    return pl.pallas_call(
        _gla_fwd_kernel,
        grid=grid,
        in_specs=[
            pl.BlockSpec((None, tb, dk), lambda b, i, h: (h, b * nt + i, 0)),
            pl.BlockSpec((None, tb, dk), lambda b, i, h: (kk + h, b * nt + i, 0)),
            pl.BlockSpec((nvb, tb, COL_B), lambda b, i, h: (kv + h, b * nt + i, 0)),
            pl.BlockSpec((None, tb, dk), lambda b, i, h: (h, b * nt + i, 0)),
        ],
        out_specs=pl.BlockSpec((nvb, tb, COL_B), lambda b, i, h: (h, b * nt + i, 0)),
        out_shape=jax.ShapeDtypeStruct((GLA_HEADS * nvb, rows, COL_B), BF16),
        scratch_shapes=[pltpu.VMEM((GLA_HEADS, nvb, dk, COL_B), F32)],
        compiler_params=pltpu.CompilerParams(
            dimension_semantics=("arbitrary", "arbitrary", "arbitrary"),
            vmem_limit_bytes=VMEM_LIMIT),
        name="gla_fwd",
    )(hmain, hmain, hmain, cum_f)


def _gla_bwd_kernel(q_ref, k_ref, v_ref, cum_ref, of_ref, z_ref, x_ref, wo_ref, gn_ref,
                    lng_ref, lnb_ref, o_ref, s_ref, y_ref, *, alpha):
    i = pl.program_id(1)
    h = pl.program_id(2)
    tb = o_ref.shape[0]
    xr = x_ref.shape[0]

    @pl.when(i == 0)
    def _reset():
        s_ref[h] = jnp.zeros(s_ref.shape[1:], F32)

    @pl.when(h == 0)
    def _zero_acc():
        o_ref[...] = jnp.zeros_like(o_ref)

    def emit(rows, o_blocks):
        nvb = len(o_blocks)
        o = [of_ref[c, rows, :].astype(F32) + o_blocks[c] for c in range(nvb)]
        ss = sum(jnp.sum(oc * oc, axis=-1, keepdims=True) for oc in o)
        rstd = lax.rsqrt(ss * (1.0 / (nvb * COL_B)) + RMS_EPS)
        for c in range(nvb):
            cols = slice(c * COL_B, (c + 1) * COL_B)
            y_ref[rows, cols] = (o[c] * rstd * gn_ref[:, cols]
                                 * z_ref[c, rows, :].astype(F32)).astype(BF16)

    _gla_sweep_tile(q_ref, k_ref, v_ref, cum_ref, s_ref, h, emit, reverse=True)

    o_ref[...] += _dot(y_ref[...], wo_ref[...])
    xrows = pl.ds(pl.multiple_of(h * xr, xr), xr)
    o_ref[xrows, :] += alpha * x_ref[...]

    @pl.when(h == GLA_HEADS - 1)
    def _finish():
        for r in range(tb // CUM_ROWS):
            rows = slice(r * CUM_ROWS, (r + 1) * CUM_ROWS)
            o_ref[rows, :] = _layer_norm_rows(o_ref[rows, :], lng_ref[...], lnb_ref[...])


def _gla_bwd(hmain, cum_b, o_f, x2, p, lng, lnb, batch, alpha):
    d = x2.shape[1]
    w_out = p["w_out"]
    key_dim = p["key_dim"]
    d_inner = w_out.shape[0]
    rows, dk, dv, tb, nt, nvb, kk, kv, kz = _gla_dims(hmain, batch, key_dim, d_inner, TB_GB)
    grid = (batch, nt, GLA_HEADS)

    def row(b, i):
        return b * nt + (nt - 1 - i)

    xr = tb // GLA_HEADS
    kernel = functools.partial(_gla_bwd_kernel, alpha=alpha)
    return pl.pallas_call(
        kernel,
        grid=grid,
        in_specs=[
            pl.BlockSpec((None, tb, dk), lambda b, i, h: (h, row(b, i), 0)),
            pl.BlockSpec((None, tb, dk), lambda b, i, h: (kk + h, row(b, i), 0)),
            pl.BlockSpec((nvb, tb, COL_B), lambda b, i, h: (kv + h, row(b, i), 0)),
            pl.BlockSpec((None, tb, dk), lambda b, i, h: (h, row(b, i), 0)),
            pl.BlockSpec((nvb, tb, COL_B), lambda b, i, h: (h, row(b, i), 0)),
            pl.BlockSpec((nvb, tb, COL_B), lambda b, i, h: (kz + h, row(b, i), 0)),
            pl.BlockSpec((xr, d), lambda b, i, h: (row(b, i) * GLA_HEADS + h, 0)),
            pl.BlockSpec((dv, d), lambda b, i, h: (h, 0)),
            pl.BlockSpec((1, dv), lambda b, i, h: (0, h)),
            pl.BlockSpec((1, d), lambda b, i, h: (0, 0)),
            pl.BlockSpec((1, d), lambda b, i, h: (0, 0)),
        ],
        out_specs=pl.BlockSpec((tb, d), lambda b, i, h: (row(b, i), 0)),
        out_shape=jax.ShapeDtypeStruct((rows, d), F32),
        scratch_shapes=[
            pltpu.VMEM((GLA_HEADS, nvb, dk, COL_B), F32),
            pltpu.VMEM((tb, dv), BF16),
        ],
        compiler_params=pltpu.CompilerParams(
            dimension_semantics=("arbitrary", "arbitrary", "arbitrary"),
            vmem_limit_bytes=VMEM_LIMIT),
        name="gla_bwd",
    )(hmain, hmain, hmain, cum_b, o_f, hmain, x2, w_out, p["gn"], lng, lnb)


def _tri_blocks(reverse):
    r = jnp.arange(CUM_ROWS)[:, None]
    c = jnp.arange(CUM_ROWS)[None, :]
    same = (r // MACRO_B) == (c // MACRO_B)
    tri = (c >= r) if reverse else (c <= r)
    return (same & tri).astype(BF16)


def _cast_block_kernel(w_ref, o_ref):
    o_ref[...] = w_ref[...].astype(BF16)


def _relayout_cols(w, cw, n_slots, src_chunk):
    k = w.shape[0]
    return pl.pallas_call(
        _cast_block_kernel,
        grid=(n_slots,),
        in_specs=[pl.BlockSpec((k, cw), lambda s: (0, src_chunk(s)))],
        out_specs=pl.BlockSpec((None, k, cw), lambda s: (s, 0, 0)),
        out_shape=jax.ShapeDtypeStruct((n_slots, k, cw), BF16),
        compiler_params=pltpu.CompilerParams(dimension_semantics=("arbitrary",)),
        name="weight_relayout",
    )(w)


def _prep_a(w_in3, layer, vg, vb, w_s, b_s, w_out):
    d = w_in3.shape[1]
    d_inner = w_out.shape[0]
    nc = d_inner // CW_A
    npv = nc // 2

    def src_chunk(s):
        pair, t = s // 2, s % 2
        return jnp.where(pair < npv, nc + s, jnp.where(t == 0, pair - npv, 2 * nc + pair - npv))

    w1 = _relayout_cols(w_in3[layer], CW_A, 3 * nc, src_chunk).reshape(npv + nc, 2, d, CW_A)
    return dict(w1=w1, w_s=w_s.astype(BF16),
                b_s=b_s[:, :, None].astype(F32), w_out=w_out.astype(BF16),
                vg=vg[None, :], vb=vb[None, :])


def _prep_b(w_in3, layer, w_g2, b_g, gn_g, w_out):
    key_dim = w_g2.shape[2]
    d_inner = w_out.shape[0]
    n_main = 2 * key_dim + 2 * d_inner
    zeros = jnp.zeros((GATE_RANK, key_dim), F32)
    w_gate = jnp.concatenate([jnp.concatenate([w_g2[0], zeros], axis=1),
                              jnp.concatenate([zeros, w_g2[1]], axis=1)], axis=0)
    return dict(w_main=_relayout_cols(w_in3[layer], CW_P, n_main // CW_P, lambda s: s),
                w_gl=w_in3[layer, :, n_main:].T.astype(BF16),
                w_gate=w_gate.astype(BF16), b_gate=b_g.reshape(1, 2 * key_dim),
                gn=gn_g[None, :], w_out=w_out.astype(BF16), key_dim=key_dim, d_inner=d_inner,
                tri_f=_tri_blocks(False), tri_b=_tri_blocks(True))


def _mixer_b_layer(x2, batch, p, lng, lnb, alpha):
    hmain, cum_f, cum_b = _proj_b(x2, p)
    o_f = _gla_fwd(hmain, cum_f, batch, p["key_dim"], p["d_inner"])
    return _gla_bwd(hmain, cum_b, o_f, x2, p, lng, lnb, batch, alpha)


def _trunk(x, layers, ln_g, ln_b, alpha):
    batch, seq, d = x.shape
    x2 = x.reshape(batch * seq, d)
    for i, (kind, p) in enumerate(layers):
        lng, lnb = ln_g[i][None, :], ln_b[i][None, :]
        if kind == "a":
            x2 = _layer_a(x2, p, lng, lnb, alpha)
        else:
            x2 = _mixer_b_layer(x2, batch, p, lng, lnb, alpha)
    return x2.reshape(batch, seq, d)


def kernel(x_prompt, x_sample, w_in_a, ln_v_g_a, ln_v_b_a, w_s_a, b_s_a, w_out_a, w_in_b, w_g2_b, b_g_b, gn_g_b, w_out_b, ln_g, ln_b):
    depth = ln_g.shape[0]
    alpha = (2 * depth) ** 0.25
    layers = []
    for i in range(depth):
        j = i // 2
        if i % 2 == 0:
            layers.append(("a", _prep_a(w_in_a, j, ln_v_g_a[j], ln_v_b_a[j], w_s_a[j], b_s_a[j], w_out_a[j])))
        else:
            layers.append(("b", _prep_b(w_in_b, j, w_g2_b[j], b_g_b[j], gn_g_b[j], w_out_b[j])))
    y_prompt = _trunk(x_prompt, layers, ln_g, ln_b, alpha)
    y_sample = _trunk(x_sample, layers, ln_g, ln_b, alpha)
    return (y_prompt, y_sample)
```

```python
import functools

import jax
import jax.numpy as jnp
from jax import lax
from jax.experimental import pallas as pl
from jax.experimental.pallas import tpu as pltpu

F32 = jnp.float32
BF16 = jnp.bfloat16

CHUNK_A = 128
N_GROUPS_A = 16
GLA_HEADS = 4
GATE_RANK = 16
GATE_TAU = 16.0
LOG2_E = 1.4426950408889634
SUB_B = 64
LN_EPS = 1e-5
RMS_EPS = 1e-6

TM_A = 1024
CW_A = 512
RB_A = 256
MXU_COLS = 256
TM_P = 1024
CW_P = 1024
TB_GF = 2048
TB_GB = 1024
MACRO_B = 256
CUM_ROWS = 256
COL_B = 256
VMEM_LIMIT = 60 * 1024 * 1024


def _dot(a, b):
    return jnp.dot(a, b, preferred_element_type=F32)


def _dot_nt(a, b):
    return lax.dot_general(a, b, (((1,), (1,)), ((), ())), preferred_element_type=F32)


def _dot_tn(a, b):
    return lax.dot_general(a, b, (((0,), (0,)), ((), ())), preferred_element_type=F32)


def _gelu_tanh(x):
    c = 0.7978845608028654
    hx = 0.5 * x
    return hx + hx * jnp.tanh(x * (c + (c * 0.044715) * (x * x)))


def _silu(x):
    hx = 0.5 * x
    return hx + hx * jnp.tanh(hx)


def _log_sigmoid(x):
    return jnp.minimum(x, 0.0) - jnp.log(1.0 + jnp.exp(-jnp.abs(x)))


def _layer_norm_rows(h, g, b):
    mu = jnp.mean(h, axis=-1, keepdims=True)
    hc = h - mu
    var = jnp.mean(hc * hc, axis=-1, keepdims=True)
    return hc * lax.rsqrt(var + LN_EPS) * g + b


def _layer_a_kernel(x_hbm, w1_ref, wo_ref, ws_ref, bs_ref, vg_ref, vb_ref,
                    lng_ref, lnb_ref, o_ref,
                    stage_ref, xb_ref, gv_ref, raw_ref, sum_ref, sq_ref, mu_ref, rstd_ref, y_ref, sem,
                    *, nc, d_inner, alpha, n_tiles):
    i = pl.program_id(0)
    j = pl.program_id(1)
    tm = o_ref.shape[0]
    cw = w1_ref.shape[2]
    gdim = d_inner // N_GROUPS_A
    npv = nc // 2
    lanes = sum_ref.shape[1]

    def x_copy(tile):
        return pltpu.make_async_copy(x_hbm.at[pl.ds(tile * tm, tm), :], stage_ref, sem)

    @pl.when((j == 0) & (i == 0))
    def _first_fetch():
        x_copy(0).start()

    def gelu_and_stats(chunk, v):
        for pc in range(cw // MXU_COLS):
            cols = slice(pc * MXU_COLS, (pc + 1) * MXU_COLS)
            gvb = _gelu_tanh(v[:, cols]).astype(BF16)
            gv_ref[chunk, :, cols] = gvb
            gv = gvb.astype(F32)
            part = gv[:, 0:lanes]
            part2 = part * part
            for l in range(1, MXU_COLS // lanes):
                blk = gv[:, l * lanes:(l + 1) * lanes]
                part = part + blk
                part2 = part2 + blk * blk
            sum_ref[...] += part
            sq_ref[...] += part2

    def phase_v_body(first):
        if not first:
            gelu_and_stats(2 * j - 1, raw_ref[...])
        gelu_and_stats(2 * j, _dot(xb_ref[...], w1_ref[0]))
        raw_ref[...] = _dot(xb_ref[...], w1_ref[1])

    @pl.when(j == 0)
    def _first_step():
        x_copy(i).wait()
        for r in range(tm // RB_A):
            rows = slice(r * RB_A, (r + 1) * RB_A)
            x = stage_ref[rows, :]
            xb_ref[rows, :] = x.astype(BF16)
            o_ref[rows, :] = alpha * x
        sum_ref[...] = jnp.zeros_like(sum_ref)
        sq_ref[...] = jnp.zeros_like(sq_ref)
        phase_v_body(first=True)

    @pl.when((j == 1) & (i + 1 < n_tiles))
    def _prefetch():
        x_copy(i + 1).start()

    @pl.when((j > 0) & (j < npv))
    def _phase_v():
        phase_v_body(first=False)

    def finish_stats():
        gelu_and_stats(nc - 1, raw_ref[...])
        mu = jnp.sum(sum_ref[...], axis=1, keepdims=True) * (1.0 / d_inner)
        var = jnp.sum(sq_ref[...], axis=1, keepdims=True) * (1.0 / d_inner) - mu * mu
        var = jnp.maximum(var, 0.0)
        mu_ref[...] = mu
        rstd_ref[...] = lax.rsqrt(var + LN_EPS)

    def mix_body(first=False):
        c = j - npv
        if first:
            finish_stats()
        xb = xb_ref[...]
        gate = _gelu_tanh(_dot(xb, w1_ref[0])) * _silu(_dot(xb, w1_ref[1]))
        vn = ((gv_ref[c].astype(F32) - mu_ref[...]) * rstd_ref[...] * vg_ref[...]
              + vb_ref[...]).astype(BF16)
        for g in range(cw // gdim):
            cols = slice(g * gdim, (g + 1) * gdim)
            wsg = ws_ref[g]
            bsg = jnp.broadcast_to(bs_ref[g], (CHUNK_A, gdim))
            for n in range(tm // CHUNK_A):
                rows = slice(n * CHUNK_A, (n + 1) * CHUNK_A)
                s = _dot(wsg, vn[rows, cols]) + bsg
                y_ref[rows, cols] = (gate[rows, cols] * s).astype(BF16)
        o_ref[...] += _dot(y_ref[...], wo_ref[...])

    @pl.when(j == npv)
    def _first_mix():
        mix_body(first=True)

    @pl.when(j > npv)
    def _mix():
        mix_body()

    @pl.when(j == npv + nc - 1)
    def _finish():
        for r in range(tm // RB_A):
            rows = slice(r * RB_A, (r + 1) * RB_A)
            o_ref[rows, :] = _layer_norm_rows(o_ref[rows, :], lng_ref[...], lnb_ref[...])


def _layer_a(x2, p, lng, lnb, alpha):
    rows, d = x2.shape
    w1, w_out = p["w1"], p["w_out"]
    d_inner = w_out.shape[0]
    cw = w1.shape[3]
    nc = d_inner // cw
    npv = nc // 2
    gpc = cw // (d_inner // N_GROUPS_A)
    tm = min(TM_A, rows)
    grid = (rows // tm, npv + nc)

    def cmix(j):
        return jnp.maximum(j - npv, 0)

    kernel = functools.partial(_layer_a_kernel, nc=nc, d_inner=d_inner, alpha=alpha, n_tiles=grid[0])
    return pl.pallas_call(
        kernel,
        grid=grid,
        in_specs=[
            pl.BlockSpec(memory_space=pl.ANY),
            pl.BlockSpec((None, 2, d, cw), lambda i, j: (j, 0, 0, 0)),
            pl.BlockSpec((cw, d), lambda i, j: (cmix(j), 0)),
            pl.BlockSpec((gpc, CHUNK_A, CHUNK_A), lambda i, j: (cmix(j), 0, 0)),
            pl.BlockSpec((gpc, CHUNK_A, 1), lambda i, j: (cmix(j), 0, 0)),
            pl.BlockSpec((1, cw), lambda i, j: (0, cmix(j))),
            pl.BlockSpec((1, cw), lambda i, j: (0, cmix(j))),
            pl.BlockSpec((1, d), lambda i, j: (0, 0)),
            pl.BlockSpec((1, d), lambda i, j: (0, 0)),
        ],
        out_specs=pl.BlockSpec((tm, d), lambda i, j: (i, 0)),
        out_shape=jax.ShapeDtypeStruct((rows, d), F32),
        scratch_shapes=[
            pltpu.VMEM((tm, d), F32),
            pltpu.VMEM((tm, d), BF16),
            pltpu.VMEM((nc, tm, cw), BF16),
            pltpu.VMEM((tm, cw), F32),
            pltpu.VMEM((tm, 128), F32),
            pltpu.VMEM((tm, 128), F32),
            pltpu.VMEM((tm, 1), F32),
            pltpu.VMEM((tm, 1), F32),
            pltpu.VMEM((tm, cw), BF16),
            pltpu.SemaphoreType.DMA,
        ],
        compiler_params=pltpu.CompilerParams(
            dimension_semantics=("arbitrary", "arbitrary"),
            vmem_limit_bytes=VMEM_LIMIT),
        name="layer_a",
    )(x2, w1, w_out, p["w_s"], p["b_s"], p["vg"], p["vb"], lng, lnb)


def _proj_b_kernel(x_ref, w_ref, wgl_ref, wg_ref, bg_ref, lf_ref, lb_ref,
                   h_ref, gf_ref, rb_ref, xb_ref, gl_ref, *, key_dim, z_chunk0):
    j = pl.program_id(1)
    tm = x_ref.shape[0]

    def project(gate_act=False):
        res = _dot(xb_ref[...], w_ref[...])
        for c in range(h_ref.shape[0]):
            blk = res[:, c * COL_B:(c + 1) * COL_B]
            h_ref[c] = (_silu(blk) if gate_act else blk).astype(BF16)

    def gates(r):
        rows = slice(r * CUM_ROWS, (r + 1) * CUM_ROWS)
        pre = _dot(gl_ref[rows, :], wg_ref[...]) + bg_ref[...]
        glog = _log_sigmoid(pre) * (LOG2_E / GATE_TAU)
        for dst, tri, cols in ((gf_ref, lf_ref, slice(0, key_dim)),
                               (rb_ref, lb_ref, slice(key_dim, 2 * key_dim))):
            cs = _dot(tri[...], glog[:, cols].astype(BF16))
            for c in range(key_dim // COL_B):
                dst[c, rows, :] = cs[:, c * COL_B:(c + 1) * COL_B]

    @pl.when(j == 0)
    def _gates_and_first_chunk():
        for r in range(tm // CUM_ROWS):
            rows = slice(r * CUM_ROWS, (r + 1) * CUM_ROWS)
            xb_ref[rows, :] = x_ref[rows, :].astype(BF16)
        gl_t = _dot_nt(wgl_ref[...], xb_ref[...])
        gl_ref[...] = gl_t.T.astype(BF16)
        for r in range(tm // CUM_ROWS):
            gates(r)
        project()

    @pl.when((j > 0) & (j < z_chunk0))
    def _qkv_chunks():
        project()

    @pl.when(j >= z_chunk0)
    def _z_chunks():
        project(gate_act=True)


def _proj_b(x2, p):
    rows, d = x2.shape
    w_main = p["w_main"]
    n_chunks, _, cw = w_main.shape
    key_dim = p["key_dim"]
    tm = min(TM_P, rows)
    grid = (rows // tm, n_chunks)
    z_chunk0 = (2 * key_dim + p["d_inner"]) // cw
    assert z_chunk0 * cw == 2 * key_dim + p["d_inner"] and z_chunk0 > 0
    kernel = functools.partial(_proj_b_kernel, key_dim=key_dim, z_chunk0=z_chunk0)
    small = [p["w_gl"], p["w_gate"], p["b_gate"], p["tri_f"], p["tri_b"]]
    return pl.pallas_call(
        kernel,
        grid=grid,
        in_specs=[
            pl.BlockSpec((tm, d), lambda i, j: (i, 0)),
            pl.BlockSpec((None, d, cw), lambda i, j: (j, 0, 0)),
        ] + [pl.BlockSpec(a.shape, lambda i, j: (0, 0)) for a in small],
        out_specs=[
            pl.BlockSpec((cw // COL_B, tm, COL_B), lambda i, j: (j, i, 0)),
            pl.BlockSpec((key_dim // COL_B, tm, COL_B), lambda i, j: (0, i, 0)),
            pl.BlockSpec((key_dim // COL_B, tm, COL_B), lambda i, j: (0, i, 0)),
        ],
        out_shape=[
            jax.ShapeDtypeStruct((n_chunks * cw // COL_B, rows, COL_B), BF16),
            jax.ShapeDtypeStruct((key_dim // COL_B, rows, COL_B), F32),
            jax.ShapeDtypeStruct((key_dim // COL_B, rows, COL_B), F32),
        ],
        scratch_shapes=[pltpu.VMEM((tm, d), BF16),
                        pltpu.VMEM((tm, p["w_gl"].shape[0]), BF16)],
        compiler_params=pltpu.CompilerParams(
            dimension_semantics=("arbitrary", "arbitrary"),
            vmem_limit_bytes=VMEM_LIMIT),
        name="proj_b",
    )(x2, w_main, *small)


def _row_to_col(vec_row, width):
    n = vec_row.shape[1]
    return jnp.broadcast_to(vec_row, (width, n)).T


def _gla_macro_chunk(q, k, v_blocks, cum, state_blocks, reverse):
    c_rows, dk = q.shape
    ns = c_rows // SUB_B
    qs = q.astype(F32) * (dk ** -0.5)
    kf = k.astype(F32)
    order = list(range(ns))[::-1] if reverse else list(range(ns))
    mid_row = SUB_B // 2 - 1 if reverse else SUB_B // 2
    end_row = 0 if reverse else SUB_B - 1

    rsl = {a: slice(a * SUB_B, (a + 1) * SUB_B) for a in range(ns)}
    r_end, q_diag, k_diag, q_off, k_off = {}, {}, {}, {}, {}
    r_prev = {}
    zero_row = jnp.zeros((1, dk), F32)
    prev = zero_row
    for a in order:
        ca = cum[rsl[a]]
        mid = ca[mid_row:mid_row + 1]
        r_end[a] = ca[end_row:end_row + 1]
        r_prev[a] = prev
        q_diag[a] = (qs[rsl[a]] * jnp.exp2(ca - mid)).astype(BF16)
        k_diag[a] = (kf[rsl[a]] * jnp.exp2(mid - ca)).astype(BF16)
        q_off[a] = qs[rsl[a]] * jnp.exp2(ca - prev)
        k_off[a] = kf[rsl[a]] * jnp.exp2(r_end[a] - ca)
        prev = r_end[a]
    r_last = prev

    ri = lax.broadcasted_iota(jnp.int32, (SUB_B, SUB_B), 0)
    ci = lax.broadcasted_iota(jnp.int32, (SUB_B, SUB_B), 1)
    keep = (ci > ri) if reverse else (ci <= ri)

    k_off_b = {a: k_off[a].astype(BF16) for a in range(ns)}
    a_rows = []
    q_state = []
    for a in range(ns):
        blocks = []
        for b in range(ns):
            if b == a:
                blk = jnp.where(keep, _dot_nt(q_diag[a], k_diag[a]), 0.0)
            elif order.index(b) < order.index(a):
                if order.index(b) + 1 == order.index(a):
                    qa = q_off[a]
                else:
                    qa = q_off[a] * jnp.exp2(r_prev[a] - r_end[b])
                blk = _dot_nt(qa.astype(BF16), k_off_b[b])
            else:
                blk = jnp.zeros((SUB_B, SUB_B), F32)
            blocks.append(blk)
        a_rows.append(jnp.concatenate(blocks, axis=1) if ns > 1 else blocks[0])
        q_state.append(q_off[a] * jnp.exp2(r_prev[a]))
    a_full = jnp.concatenate(a_rows, axis=0) if ns > 1 else a_rows[0]
    q_st = jnp.concatenate(q_state, axis=0) if ns > 1 else q_state[0]

    a_b = a_full.astype(BF16)
    q_b = q_st.astype(BF16)
    k_dec = [k_off[a] * jnp.exp2(r_last - r_end[a]) for a in range(ns)]
    k_dec = (jnp.concatenate(k_dec, axis=0) if ns > 1 else k_dec[0]).astype(BF16)
    dcol = _row_to_col(jnp.exp2(r_last), 128)
    dcol = jnp.concatenate([dcol] * (v_blocks[0].shape[1] // 128), axis=1)
    o_blocks, new_state = [], []
    for vc, sc in zip(v_blocks, state_blocks):
        o_blocks.append(_dot(a_b, vc) + _dot(q_b, sc.astype(BF16)))
        new_state.append(sc * dcol + _dot_tn(k_dec, vc))
    return o_blocks, new_state


def _gla_sweep_tile(q_ref, k_ref, v_ref, cum_ref, s_ref, h, emit, reverse):
    tb = q_ref.shape[0]
    nvb = v_ref.shape[0]
    nm = tb // MACRO_B
    chunks = range(nm - 1, -1, -1) if reverse else range(nm)
    for m in chunks:
        rows = slice(m * MACRO_B, (m + 1) * MACRO_B)
        o_blocks, s_new = _gla_macro_chunk(
            q_ref[rows, :], k_ref[rows, :], [v_ref[c, rows, :] for c in range(nvb)], cum_ref[rows, :],
            [s_ref[h, c] for c in range(nvb)], reverse)
        for c in range(nvb):
            s_ref[h, c] = s_new[c]
        emit(rows, o_blocks)


def _gla_fwd_kernel(q_ref, k_ref, v_ref, cum_ref, o_ref, s_ref):
    i = pl.program_id(1)
    h = pl.program_id(2)

    @pl.when(i == 0)
    def _reset():
        s_ref[h] = jnp.zeros(s_ref.shape[1:], F32)

    def emit(rows, o_blocks):
        for c, o in enumerate(o_blocks):
            o_ref[c, rows, :] = o.astype(o_ref.dtype)

    _gla_sweep_tile(q_ref, k_ref, v_ref, cum_ref, s_ref, h, emit, reverse=False)


def _gla_dims(hmain, batch, key_dim, d_inner, tile):
    rows = hmain.shape[1]
    dk = key_dim // GLA_HEADS
    dv = d_inner // GLA_HEADS
    assert dk == COL_B and dv % COL_B == 0
    tb = min(tile, rows // batch)
    nt = rows // batch // tb
    nvb = dv // COL_B
    kk = key_dim // COL_B
    kv = 2 * key_dim // (nvb * COL_B)
    kz = (2 * key_dim + d_inner) // (nvb * COL_B)
    return rows, dk, dv, tb, nt, nvb, kk, kv, kz


def _gla_fwd(hmain, cum_f, batch, key_dim, d_inner):
    rows, dk, dv, tb, nt, nvb, kk, kv, _ = _gla_dims(hmain, batch, key_dim, d_inner, TB_GF)
    grid = (batch, nt, GLA_HEADS)
    return pl.pallas_call(
        _gla_fwd_kernel,
        grid=grid,
        in_specs=[
            pl.BlockSpec((None, tb, dk), lambda b, i, h: (h, b * nt + i, 0)),
            pl.BlockSpec((None, tb, dk), lambda b, i, h: (kk + h, b * nt + i, 0)),
            pl.BlockSpec((nvb, tb, COL_B), lambda b, i, h: (kv + h, b * nt + i, 0)),
            pl.BlockSpec((None, tb, dk), lambda b, i, h: (h, b * nt + i, 0)),
        ],
        out_specs=pl.BlockSpec((nvb, tb, COL_B), lambda b, i, h: (h, b * nt + i, 0)),
        out_shape=jax.ShapeDtypeStruct((GLA_HEADS * nvb, rows, COL_B), BF16),
        scratch_shapes=[pltpu.VMEM((GLA_HEADS, nvb, dk, COL_B), F32)],
        compiler_params=pltpu.CompilerParams(
            dimension_semantics=("arbitrary", "arbitrary", "arbitrary"),
            vmem_limit_bytes=VMEM_LIMIT),
        name="gla_fwd",
    )(hmain, hmain, hmain, cum_f)


def _gla_bwd_kernel(q_ref, k_ref, v_ref, cum_ref, of_ref, z_ref, x_ref, wo_ref, gn_ref,
                    lng_ref, lnb_ref, o_ref, s_ref, y_ref, *, alpha):
    i = pl.program_id(1)
    h = pl.program_id(2)
    tb = o_ref.shape[0]
    xr = x_ref.shape[0]

    @pl.when(i == 0)
    def _reset():
        s_ref[h] = jnp.zeros(s_ref.shape[1:], F32)

    @pl.when(h == 0)
    def _zero_acc():
        o_ref[...] = jnp.zeros_like(o_ref)

    def emit(rows, o_blocks):
        nvb = len(o_blocks)
        o = [of_ref[c, rows, :].astype(F32) + o_blocks[c] for c in range(nvb)]
        ss = sum(jnp.sum(oc * oc, axis=-1, keepdims=True) for oc in o)
        rstd = lax.rsqrt(ss * (1.0 / (nvb * COL_B)) + RMS_EPS)
        for c in range(nvb):
            cols = slice(c * COL_B, (c + 1) * COL_B)
            y_ref[rows, cols] = (o[c] * rstd * gn_ref[:, cols]
                                 * z_ref[c, rows, :].astype(F32)).astype(BF16)

    _gla_sweep_tile(q_ref, k_ref, v_ref, cum_ref, s_ref, h, emit, reverse=True)

    o_ref[...] += _dot(y_ref[...], wo_ref[...])
    xrows = pl.ds(pl.multiple_of(h * xr, xr), xr)
    o_ref[xrows, :] += alpha * x_ref[...]

    @pl.when(h == GLA_HEADS - 1)
    def _finish():
        for r in range(tb // CUM_ROWS):
            rows = slice(r * CUM_ROWS, (r + 1) * CUM_ROWS)
            o_ref[rows, :] = _layer_norm_rows(o_ref[rows, :], lng_ref[...], lnb_ref[...])


def _gla_bwd(hmain, cum_b, o_f, x2, p, lng, lnb, batch, alpha):
    d = x2.shape[1]
    w_out = p["w_out"]
    key_dim = p["key_dim"]
    d_inner = w_out.shape[0]
    rows, dk, dv, tb, nt, nvb, kk, kv, kz = _gla_dims(hmain, batch, key_dim, d_inner, TB_GB)
    grid = (batch, nt, GLA_HEADS)

    def row(b, i):
        return b * nt + (nt - 1 - i)

    xr = tb // GLA_HEADS
    kernel = functools.partial(_gla_bwd_kernel, alpha=alpha)
    return pl.pallas_call(
        kernel,
        grid=grid,
        in_specs=[
            pl.BlockSpec((None, tb, dk), lambda b, i, h: (h, row(b, i), 0)),
            pl.BlockSpec((None, tb, dk), lambda b, i, h: (kk + h, row(b, i), 0)),
            pl.BlockSpec((nvb, tb, COL_B), lambda b, i, h: (kv + h, row(b, i), 0)),
            pl.BlockSpec((None, tb, dk), lambda b, i, h: (h, row(b, i), 0)),
            pl.BlockSpec((nvb, tb, COL_B), lambda b, i, h: (h, row(b, i), 0)),
            pl.BlockSpec((nvb, tb, COL_B), lambda b, i, h: (kz + h, row(b, i), 0)),
            pl.BlockSpec((xr, d), lambda b, i, h: (row(b, i) * GLA_HEADS + h, 0)),
            pl.BlockSpec((dv, d), lambda b, i, h: (h, 0)),
            pl.BlockSpec((1, dv), lambda b, i, h: (0, h)),
            pl.BlockSpec((1, d), lambda b, i, h: (0, 0)),
            pl.BlockSpec((1, d), lambda b, i, h: (0, 0)),
        ],
        out_specs=pl.BlockSpec((tb, d), lambda b, i, h: (row(b, i), 0)),
        out_shape=jax.ShapeDtypeStruct((rows, d), F32),
        scratch_shapes=[
            pltpu.VMEM((GLA_HEADS, nvb, dk, COL_B), F32),
            pltpu.VMEM((tb, dv), BF16),
        ],
        compiler_params=pltpu.CompilerParams(
            dimension_semantics=("arbitrary", "arbitrary", "arbitrary"),
            vmem_limit_bytes=VMEM_LIMIT),
        name="gla_bwd",
    )(hmain, hmain, hmain, cum_b, o_f, hmain, x2, w_out, p["gn"], lng, lnb)


def _tri_blocks(reverse):
    r = jnp.arange(CUM_ROWS)[:, None]
    c = jnp.arange(CUM_ROWS)[None, :]
    same = (r // MACRO_B) == (c // MACRO_B)
    tri = (c >= r) if reverse else (c <= r)
    return (same & tri).astype(BF16)


def _cast_block_kernel(w_ref, o_ref):
    o_ref[...] = w_ref[...].astype(BF16)


def _relayout_cols(w, cw, n_slots, src_chunk):
    k = w.shape[0]
    return pl.pallas_call(
        _cast_block_kernel,
        grid=(n_slots,),
        in_specs=[pl.BlockSpec((k, cw), lambda s: (0, src_chunk(s)))],
        out_specs=pl.BlockSpec((None, k, cw), lambda s: (s, 0, 0)),
        out_shape=jax.ShapeDtypeStruct((n_slots, k, cw), BF16),
        compiler_params=pltpu.CompilerParams(dimension_semantics=("arbitrary",)),
        name="weight_relayout",
    )(w)


def _prep_a(w_in3, layer, vg, vb, w_s, b_s, w_out):
    d = w_in3.shape[1]
    d_inner = w_out.shape[0]
    nc = d_inner // CW_A
    npv = nc // 2

    def src_chunk(s):
        pair, t = s // 2, s % 2
        return jnp.where(pair < npv, nc + s, jnp.where(t == 0, pair - npv, 2 * nc + pair - npv))

    w1 = _relayout_cols(w_in3[layer], CW_A, 3 * nc, src_chunk).reshape(npv + nc, 2, d, CW_A)
    return dict(w1=w1, w_s=w_s.astype(BF16),
                b_s=b_s[:, :, None].astype(F32), w_out=w_out.astype(BF16),
                vg=vg[None, :], vb=vb[None, :])


def _prep_b(w_in3, layer, w_g2, b_g, gn_g, w_out):
    key_dim = w_g2.shape[2]
    d_inner = w_out.shape[0]
    n_main = 2 * key_dim + 2 * d_inner
    zeros = jnp.zeros((GATE_RANK, key_dim), F32)
    w_gate = jnp.concatenate([jnp.concatenate([w_g2[0], zeros], axis=1),
                              jnp.concatenate([zeros, w_g2[1]], axis=1)], axis=0)
    return dict(w_main=_relayout_cols(w_in3[layer], CW_P, n_main // CW_P, lambda s: s),
                w_gl=w_in3[layer, :, n_main:].T.astype(BF16),
                w_gate=w_gate.astype(BF16), b_gate=b_g.reshape(1, 2 * key_dim),
                gn=gn_g[None, :], w_out=w_out.astype(BF16), key_dim=key_dim, d_inner=d_inner,
                tri_f=_tri_blocks(False), tri_b=_tri_blocks(True))


def _mixer_b_layer(x2, batch, p, lng, lnb, alpha):
    hmain, cum_f, cum_b = _proj_b(x2, p)
    o_f = _gla_fwd(hmain, cum_f, batch, p["key_dim"], p["d_inner"])
    return _gla_bwd(hmain, cum_b, o_f, x2, p, lng, lnb, batch, alpha)


def _trunk(x, layers, ln_g, ln_b, alpha):
    batch, seq, d = x.shape
    x2 = x.reshape(batch * seq, d)
    for i, (kind, p) in enumerate(layers):
        lng, lnb = ln_g[i][None, :], ln_b[i][None, :]
        if kind == "a":
            x2 = _layer_a(x2, p, lng, lnb, alpha)
        else:
            x2 = _mixer_b_layer(x2, batch, p, lng, lnb, alpha)
    return x2.reshape(batch, seq, d)


def kernel(x_prompt, x_sample, w_in_a, ln_v_g_a, ln_v_b_a, w_s_a, b_s_a, w_out_a, w_in_b, w_g2_b, b_g_b, gn_g_b, w_out_b, ln_g, ln_b):
    depth = ln_g.shape[0]
    alpha = (2 * depth) ** 0.25
    layers = []
    for i in range(depth):
        j = i // 2
        if i % 2 == 0:
            layers.append(("a", _prep_a(w_in_a, j, ln_v_g_a[j], ln_v_b_a[j], w_s_a[j], b_s_a[j], w_out_a[j])))
        else:
            layers.append(("b", _prep_b(w_in_b, j, w_g2_b[j], b_g_b[j], gn_g_b[j], w_out_b[j])))
    y_prompt = _trunk(x_prompt, layers, ln_g, ln_b, alpha)
    y_sample = _trunk(x_sample, layers, ln_g, ln_b, alpha)
    return (y_prompt, y_sample)
```
